```python
import jax, jax.numpy as jnp
from jax import lax
import numpy as np

D_MODEL = 4096
BATCH = 2
SEQ = 8192
DEPTH = 4

N_META = 16
CHUNK = 64
EPS = 1e-6
H_A = 4
V_A = D_MODEL // 4
QK_A = V_A // 2
DV_A = V_A // H_A
DK_A = QK_A // H_A
CONV_W = 4
F_BIAS = 3.0
H_B = 4
V_B = D_MODEL // 4
QK_B = V_B // 2
DV_B = V_B // H_B
DK_B = QK_B // H_B
GATE_RANK = 16
GATE_TAU = 16.0
D_FF = 4 * D_MODEL
SPLITS = (QK_A, QK_A, V_A, V_A, H_A, H_A, QK_B, QK_B, V_B, V_B, GATE_RANK, D_MODEL, D_MODEL)
P_IN = int(sum(SPLITS))
SPLIT_POINTS = tuple(int(s) for s in np.cumsum(SPLITS)[:-1])

kernel_name = 'mlstm_gla_griffin_merge_hybrid'


def rmsnorm(x, g):
    xf = x.astype(jnp.float32)
    y = xf * lax.rsqrt(jnp.mean(xf * xf, axis=-1, keepdims=True) + EPS)
    return (y * g.astype(jnp.float32)).astype(x.dtype)


def causal_dwconv(x, w):
    return lax.conv_general_dilated(x, w[:, None, :].astype(x.dtype), window_strides=(1,), padding=[(CONV_W - 1, 0)], dimension_numbers=('NWC', 'WIO', 'NWC'), feature_group_count=x.shape[-1])


def to_chunks(a):
    bsz, t = a.shape[0], a.shape[1]
    a = a.reshape((bsz, t // CHUNK, CHUNK) + a.shape[2:])
    return a.transpose((1, 0, 3, 2) + tuple(range(4, a.ndim)))


def from_chunks(y):
    nc, bsz, heads, l, d = y.shape
    return y.transpose(1, 0, 3, 2, 4).reshape(bsz, nc * l, heads, d)


def mlstm_chunked(q, k, v, i_pre, f_pre, valid):
    bsz, _, heads, dk = q.shape
    dv = v.shape[-1]
    causal = jnp.tril(jnp.ones((CHUNK, CHUNK), dtype=bool))
    log_i = jnp.where(valid[None, :, None], i_pre, -jnp.inf)
    log_f = jnp.where(valid[None, :, None], jax.nn.log_sigmoid(f_pre), 0.0)

    def step(carry, inp):
        c_st, n_st, m_st = carry
        qc, kc, vc, li, lf = inp
        b = jnp.cumsum(lf, axis=-1)
        log_d = jnp.where(causal, b[..., :, None] - b[..., None, :] + li[..., None, :], -jnp.inf)
        m_t = jnp.maximum(b + m_st[..., None], jnp.max(log_d, axis=-1))
        s = jnp.einsum('bhtd,bhsd->bhts', qc, kc) * jnp.exp(log_d - m_t[..., None])
        carry_w = jnp.exp(b + m_st[..., None] - m_t)
        num = jnp.einsum('bhts,bhsv->bhtv', s, vc) + carry_w[..., None] * jnp.einsum('bhtd,bhvd->bhtv', qc, c_st)
        den = jnp.sum(s, axis=-1) + carry_w * jnp.einsum('bhtd,bhd->bht', qc, n_st)
        h = num / jnp.maximum(jnp.abs(den), jnp.exp(-m_t))[..., None]
        g = b[..., -1]
        log_w = g[..., None] - b + li
        m_new = jnp.maximum(g + m_st, jnp.max(log_w, axis=-1))
        w = jnp.exp(log_w - m_new[..., None])
        decay = jnp.exp(g + m_st - m_new)
        c_st = decay[..., None, None] * c_st + jnp.einsum('bhs,bhsv,bhsd->bhvd', w, vc, kc)
        n_st = decay[..., None] * n_st + jnp.einsum('bhs,bhsd->bhd', w, kc)
        return (c_st, n_st, m_new), h

    init = (jnp.zeros((bsz, heads, dv, dk), jnp.float32), jnp.zeros((bsz, heads, dk), jnp.float32), jnp.zeros((bsz, heads), jnp.float32))
    _, hs = lax.scan(step, init, (to_chunks(q), to_chunks(k), to_chunks(v), to_chunks(log_i), to_chunks(log_f)))
    return from_chunks(hs)


def gla_chunked(q, k, v, log_a):
    bsz, _, heads, dk = q.shape
    dv = v.shape[-1]
    causal = jnp.tril(jnp.ones((CHUNK, CHUNK), dtype=bool))

    def step(s_st, inp):
        qc, kc, vc, la = inp
        cum = jnp.cumsum(la, axis=2)
        rel = jnp.exp(jnp.where(causal[:, :, None], cum[:, :, :, None, :] - cum[:, :, None, :, :], -jnp.inf))
        att = jnp.einsum('bhtc,bhsc,bhtsc->bhts', qc, kc, rel)
        o = jnp.einsum('bhts,bhsv->bhtv', att, vc) + jnp.einsum('bhtc,bhcv->bhtv', qc * jnp.exp(cum), s_st)
        last = cum[:, :, -1:, :]
        s_st = jnp.exp(last[:, :, 0])[..., None] * s_st + jnp.einsum('bhsc,bhsv->bhcv', kc * jnp.exp(last - cum), vc)
        return s_st, o

    init = jnp.zeros((bsz, heads, dk, dv), jnp.float32)
    _, os_ = lax.scan(step, init, (to_chunks(q), to_chunks(k), to_chunks(v), to_chunks(log_a)))
    return from_chunks(os_)


def hybrid_mixer(h, w_in, conv_qk, b_if, w_gla_gate, b_gla_gate, norm_gla, w_br_a, w_br_b, w_out):
    bsz, n, _ = h.shape
    dt = h.dtype
    f32 = jnp.float32
    proj = h @ w_in
    qa, ka, va, oa, ia, fa, qb, kb, vb, gb, zb, ga, gbr = jnp.split(proj, SPLIT_POINTS, axis=-1)
    pad = CHUNK - N_META
    t_len = pad + n
    valid = jnp.arange(t_len) >= pad

    def prep(a, heads):
        a = jnp.pad(a.astype(f32), ((0, 0), (pad, 0), (0, 0)))
        return a.reshape(bsz, t_len, heads, -1)

    qka = jax.nn.silu(causal_dwconv(jnp.concatenate([qa, ka], axis=-1), conv_qk))
    qa, ka = jnp.split(qka, 2, axis=-1)
    i_pre = prep(ia + b_if[:H_A].astype(dt), H_A)[..., 0]
    f_pre = prep(fa + b_if[H_A:].astype(dt), H_A)[..., 0]
    h_a = mlstm_chunked(prep(qa, H_A), prep(ka, H_A) * (DK_A ** -0.5), prep(va, H_A), i_pre, f_pre, valid)
    y_a = h_a[:, pad:].reshape(bsz, n, V_A) * jax.nn.sigmoid(oa.astype(f32))

    log_a = jax.nn.log_sigmoid((zb @ w_gla_gate + b_gla_gate.astype(dt)).astype(f32)) / GATE_TAU
    o_b = gla_chunked(prep(qb, H_B) * (DK_B ** -0.5), prep(kb, H_B), prep(vb, H_B), prep(log_a, H_B))[:, pad:]
    o_b = o_b * lax.rsqrt(jnp.mean(o_b * o_b, axis=-1, keepdims=True) + EPS)
    y_b = o_b.reshape(bsz, n, V_B) * norm_gla.astype(f32) * jax.nn.silu(gb.astype(f32))

    merged = jax.nn.sigmoid(ga) * (y_a.astype(dt) @ w_br_a) + jax.nn.sigmoid(gbr) * (y_b.astype(dt) @ w_br_b)
    return merged @ w_out


def setup_inputs(seed: int = 0) -> dict:
    key = jax.random.key(seed)
    ks = jax.random.split(key, 20)

    def nrm(k, shape, scale):
        return jax.random.normal(k, shape, jnp.float32) * scale

    return {
        'x': nrm(ks[0], (BATCH, SEQ, D_MODEL), 1.0),
        'meta': nrm(ks[1], (N_META, D_MODEL), 1.0),
        'norm_mix': 1.0 + nrm(ks[2], (DEPTH, D_MODEL), 0.02),
        'w_in': nrm(ks[3], (DEPTH, D_MODEL, P_IN), D_MODEL ** -0.5),
        'conv_qk': nrm(ks[4], (DEPTH, CONV_W, 2 * QK_A), CONV_W ** -0.5),
        'b_if': jnp.concatenate([nrm(ks[5], (DEPTH, H_A), 0.1), F_BIAS + nrm(ks[6], (DEPTH, H_A), 0.1)], axis=-1),
        'w_gla_gate': nrm(ks[7], (DEPTH, GATE_RANK, QK_B), GATE_RANK ** -0.5),
        'b_gla_gate': nrm(ks[8], (DEPTH, QK_B), 0.1),
        'norm_gla': 1.0 + nrm(ks[9], (DEPTH, V_B), 0.02),
        'w_br_a': nrm(ks[10], (DEPTH, V_A, D_MODEL), V_A ** -0.5),
        'w_br_b': nrm(ks[11], (DEPTH, V_B, D_MODEL), V_B ** -0.5),
        'w_out': nrm(ks[12], (DEPTH, D_MODEL, D_MODEL), D_MODEL ** -0.5),
        'norm_mlp': 1.0 + nrm(ks[13], (DEPTH, D_MODEL), 0.02),
        'w_up': nrm(ks[14], (DEPTH, D_MODEL, D_FF), D_MODEL ** -0.5),
        'w_down': nrm(ks[15], (DEPTH, D_FF, D_MODEL), D_FF ** -0.5),
        'norm_final': 1.0 + nrm(ks[16], (D_MODEL,), 0.02),
    }


def reference(x, meta, norm_mix, w_in, conv_qk, b_if, w_gla_gate, b_gla_gate, norm_gla, w_br_a, w_br_b, w_out, norm_mlp, w_up, w_down, norm_final):
    bsz = x.shape[0]
    h = jnp.concatenate([jnp.broadcast_to(meta.astype(x.dtype)[None], (bsz, N_META, D_MODEL)), x], axis=1)
    for l in range(DEPTH):
        h = h + hybrid_mixer(rmsnorm(h, norm_mix[l]), w_in[l], conv_qk[l], b_if[l], w_gla_gate[l], b_gla_gate[l], norm_gla[l], w_br_a[l], w_br_b[l], w_out[l])
        u = jax.nn.relu(rmsnorm(h, norm_mlp[l]) @ w_up[l])
        h = h + (u * u) @ w_down[l]
    return rmsnorm(h[:, N_META:], norm_final)
```

```python
import functools

import jax
import jax.numpy as jnp
from jax import lax
from jax.experimental import pallas as pl
from jax.experimental.pallas import tpu as pltpu

F32 = jnp.float32
BF16 = jnp.bfloat16

N_META = 16
EPS = 1e-6
HEADS = 4
DK = 128
DV = 256
QK = HEADS * DK
VW = HEADS * DV
CONV_W = 4
GATE_RANK = 16
GATE_TAU = 16.0
SMALL_W = 128
CONV_HALO = 8

MLSTM_CHUNK = 128
GLA_CHUNK = 64
T_ALIGN = 128

VMEM_LIMIT_BYTES = 56 * 1024 * 1024


def _cparams(sem):
    return pltpu.CompilerParams(dimension_semantics=sem, vmem_limit_bytes=VMEM_LIMIT_BYTES)


def _tile(n, pref):
    if n <= pref:
        return n
    t = (pref // 128) * 128
    while t >= 128:
        if n % t == 0:
            return t
        t -= 128
    raise ValueError(f"no 128-aligned tile for {n}")


def _sigmoid(x):
    return 1.0 / (1.0 + jnp.exp(-x))


def _log_sigmoid(x):
    return jnp.minimum(x, 0.0) - jnp.log1p(jnp.exp(-jnp.abs(x)))


def _split3(x):
    hi = x.astype(BF16)
    r = x - hi.astype(F32)
    mid = r.astype(BF16)
    lo = (r - mid.astype(F32)).astype(BF16)
    return hi, mid, lo


def _dot(a, b):
    return jnp.dot(a, b, preferred_element_type=F32)


def _tri_left(tri, x):
    hi, mid, lo = _split3(x)
    return (_dot(tri, lo) + _dot(tri, mid)) + _dot(tri, hi)


def _tri_right(x, tri):
    hi, mid, lo = _split3(x)
    return (_dot(lo, tri) + _dot(mid, tri)) + _dot(hi, tri)


def _tri(n, lower):
    r = lax.broadcasted_iota(jnp.int32, (n, n), 0)
    c = lax.broadcasted_iota(jnp.int32, (n, n), 1)
    keep = (r >= c) if lower else (r <= c)
    return jnp.where(keep, 1.0, 0.0).astype(BF16)


def _rmsnorm_kernel(x_ref, g_ref, o_ref):
    x = x_ref[...]
    ms = jnp.mean(x * x, axis=-1, keepdims=True)
    o_ref[...] = (x * lax.rsqrt(ms + EPS) * g_ref[...]).astype(o_ref.dtype)


def rmsnorm(x, g, out_dtype, rows=256):
    m, d = x.shape
    tr = _tile(m, rows)
    return pl.pallas_call(
        _rmsnorm_kernel,
        grid=(m // tr,),
        in_specs=[pl.BlockSpec((tr, d), lambda i: (i, 0)),
                  pl.BlockSpec((1, d), lambda i: (0, 0))],
        out_specs=pl.BlockSpec((tr, d), lambda i: (i, 0)),
        out_shape=jax.ShapeDtypeStruct((m, d), out_dtype),
        compiler_params=_cparams(("parallel",)),
        name="rmsnorm",
    )(x, g.reshape(1, d).astype(F32))


def _mm_kernel(x_ref, w_ref, o_ref):
    o_ref[...] = _dot(x_ref[...], w_ref[...]).astype(o_ref.dtype)


def _mm_relu2_kernel(x_ref, w_ref, o_ref):
    u = jnp.maximum(_dot(x_ref[...], w_ref[...]), 0.0)
    o_ref[...] = (u * u).astype(o_ref.dtype)


def _mm_residual_kernel(x_ref, w_ref, h_ref, o_ref):
    o_ref[...] = h_ref[...] + _dot(x_ref[...], w_ref[...])


def _matmul(body, x, w, out_dtype, tm, tn, residual=None, name="matmul"):
    m, k = x.shape
    n = w.shape[1]
    tm, tn = _tile(m, tm), _tile(n, tn)
    in_specs = [pl.BlockSpec((tm, k), lambda i, j: (i, 0)),
                pl.BlockSpec((k, tn), lambda i, j: (0, j))]
    args = [x, w]
    if residual is not None:
        in_specs.append(pl.BlockSpec((tm, tn), lambda i, j: (i, j)))
        args.append(residual)
    return pl.pallas_call(
        body,
        grid=(m // tm, n // tn),
        in_specs=in_specs,
        out_specs=pl.BlockSpec((tm, tn), lambda i, j: (i, j)),
        out_shape=jax.ShapeDtypeStruct((m, n), out_dtype),
        compiler_params=_cparams(("parallel", "parallel")),
        name=name,
    )(*args)


def _down_kernel(u_ref, w_ref, h_ref, o_ref, acc_ref):
    kk = pl.program_id(2)

    @pl.when(kk == 0)
    def _():
        acc_ref[...] = h_ref[...]

    acc_ref[...] += _dot(u_ref[...], w_ref[...])

    @pl.when(kk == pl.num_programs(2) - 1)
    def _():
        o_ref[...] = acc_ref[...]


def down_proj(u, w, h, tm, tn, tk):
    m, k = u.shape
    n = w.shape[1]
    tm, tn, tk = _tile(m, tm), _tile(n, tn), _tile(k, tk)
    return pl.pallas_call(
        _down_kernel,
        grid=(m // tm, n // tn, k // tk),
        in_specs=[pl.BlockSpec((tm, tk), lambda i, j, kk: (i, kk)),
                  pl.BlockSpec((tk, tn), lambda i, j, kk: (kk, j)),
                  pl.BlockSpec((tm, tn), lambda i, j, kk: (i, j))],
        out_specs=pl.BlockSpec((tm, tn), lambda i, j, kk: (i, j)),
        out_shape=jax.ShapeDtypeStruct((m, n), F32),
        scratch_shapes=[pltpu.VMEM((tm, tn), F32)],
        compiler_params=_cparams(("parallel", "parallel", "arbitrary")),
        name="down_proj",
    )(u, w, h)


def _merge_kernel(ya_ref, yb_ref, wa_ref, wb_ref, ga_ref, gb_ref, o_ref):
    a = _dot(ya_ref[...], wa_ref[...])
    b = _dot(yb_ref[...], wb_ref[...])
    o_ref[...] = (_sigmoid(ga_ref[...]) * a + _sigmoid(gb_ref[...]) * b).astype(o_ref.dtype)


def merge(ya, yb, wa, wb, proj, ga_col, gb_col, tm, tn):
    m, k = ya.shape
    n = wa.shape[1]
    tm, tn = _tile(m, tm), _tile(n, tn)
    ga_blk, gb_blk = ga_col // tn, gb_col // tn
    assert ga_blk * tn == ga_col and gb_blk * tn == gb_col
    return pl.pallas_call(
        _merge_kernel,
        grid=(m // tm, n // tn),
        in_specs=[pl.BlockSpec((tm, k), lambda i, j: (i, 0)),
                  pl.BlockSpec((tm, k), lambda i, j: (i, 0)),
                  pl.BlockSpec((k, tn), lambda i, j: (0, j)),
                  pl.BlockSpec((k, tn), lambda i, j: (0, j)),
                  pl.BlockSpec((tm, tn), lambda i, j: (i, ga_blk + j)),
                  pl.BlockSpec((tm, tn), lambda i, j: (i, gb_blk + j))],
        out_specs=pl.BlockSpec((tm, tn), lambda i, j: (i, j)),
        out_shape=jax.ShapeDtypeStruct((m, n), BF16),
        compiler_params=_cparams(("parallel", "parallel")),
        name="merge",
    )(ya, yb, wa, wb, proj, proj)


def _mlstm_kernel(qk_ref, v_ref, og_ref, gc_ref, gr_ref, convw_ref, bc_ref, br_ref,
                  o_ref, xbuf, c_st, n_st, m_st):
    L = MLSTM_CHUNK

    @pl.when(pl.program_id(1) == 0)
    def _():
        xbuf[0:CONV_HALO, :] = jnp.zeros((CONV_HALO, 2 * QK), F32)
        c_st[...] = jnp.zeros_like(c_st)
        n_st[...] = jnp.zeros_like(n_st)
        m_st[...] = jnp.zeros_like(m_st)

    xbuf[CONV_HALO:CONV_HALO + L, :] = qk_ref[0]
    w = convw_ref[...]
    acc = w[CONV_W - 1:CONV_W, :] * xbuf[CONV_HALO:CONV_HALO + L, :]
    for j in range(CONV_W - 1):
        back = CONV_W - 1 - j
        acc = acc + w[j:j + 1, :] * xbuf[CONV_HALO - back:CONV_HALO - back + L, :]
    xbuf[0:CONV_HALO, :] = xbuf[L:L + CONV_HALO, :]
    qk = acc * _sigmoid(acc)

    g_col = gc_ref[0] + bc_ref[...]
    g_row = gr_ref[0] + br_ref[...]
    b_col = _tri_left(_tri(L, True), _log_sigmoid(g_col))
    b_row = _tri_right(_log_sigmoid(g_row), _tri(L, False))

    row = lax.broadcasted_iota(jnp.int32, (L, L), 0)
    col = lax.broadcasted_iota(jnp.int32, (L, L), 1)
    causal = row >= col

    for h in range(HEADS):
        q_f = qk[:, h * DK:(h + 1) * DK]
        q = q_f.astype(BF16)
        k_f = qk[:, QK + h * DK:QK + (h + 1) * DK] * (DK ** -0.5)
        k = k_f.astype(BF16)
        v_f = v_ref[0, :, h * DV:(h + 1) * DV]
        v = v_f.astype(BF16)
        li_c = g_col[:, h:h + 1]
        li_r = g_row[h:h + 1, :]
        b_c = b_col[:, HEADS + h:HEADS + h + 1]
        b_r = b_row[HEADS + h:HEADS + h + 1, :]
        m_prev = m_st[h:h + 1, 0:1]
        ct = c_st[h]
        n_row = n_st[h:h + 1, :]

        log_d = jnp.where(causal, b_c - b_r + li_r, -jnp.inf)
        m_t = jnp.maximum(b_c + m_prev, jnp.max(log_d, axis=1, keepdims=True))
        s = lax.dot_general(q, k, (((1,), (1,)), ((), ())), preferred_element_type=F32)
        s = s * jnp.exp(log_d - m_t)
        carry_w = jnp.exp(b_c + m_prev - m_t)
        num = _dot(s.astype(BF16), v) + carry_w * _dot(q, ct.astype(BF16))
        qn = jnp.sum(q_f * n_row, axis=1, keepdims=True)
        den = jnp.sum(s, axis=1, keepdims=True) + carry_w * qn
        hh = num / jnp.maximum(jnp.abs(den), jnp.exp(-m_t))
        og = og_ref[0, :, h * DV:(h + 1) * DV]
        o_ref[0, :, h * DV:(h + 1) * DV] = (hh * _sigmoid(og)).astype(o_ref.dtype)

        g_tot = b_c[L - 1:L, :]
        log_w = g_tot - b_c + li_c
        m_new = jnp.maximum(g_tot + m_prev, jnp.max(log_w, axis=0, keepdims=True))
        wgt = jnp.exp(log_w - m_new)
        decay = jnp.exp(g_tot + m_prev - m_new)
        kv = lax.dot_general(k, (wgt * v_f).astype(BF16), (((0,), (0,)), ((), ())),
                             preferred_element_type=F32)
        c_st[h] = decay * ct + kv
        n_st[h:h + 1, :] = decay * n_row + jnp.sum(wgt * k_f, axis=0, keepdims=True)
        m_st[h:h + 1, :] = jnp.broadcast_to(m_new, (1, m_st.shape[1]))


def mlstm_mixer(proj, small, small_t, conv_w, bias_col, bias_row, qk_col, v_col, og_col):
    bsz, t, _ = proj.shape
    L = MLSTM_CHUNK
    assert t % L == 0
    qk_blk, v_blk, og_blk = qk_col // (2 * QK), v_col // VW, og_col // VW
    return pl.pallas_call(
        _mlstm_kernel,
        grid=(bsz, t // L),
        in_specs=[pl.BlockSpec((1, L, 2 * QK), lambda b, c: (b, c, qk_blk)),
                  pl.BlockSpec((1, L, VW), lambda b, c: (b, c, v_blk)),
                  pl.BlockSpec((1, L, VW), lambda b, c: (b, c, og_blk)),
                  pl.BlockSpec((1, L, SMALL_W), lambda b, c: (b, c, 0)),
                  pl.BlockSpec((1, 8, L), lambda b, c: (b, 0, c)),
                  pl.BlockSpec((CONV_W, 2 * QK), lambda b, c: (0, 0)),
                  pl.BlockSpec((1, SMALL_W), lambda b, c: (0, 0)),
                  pl.BlockSpec((8, L), lambda b, c: (0, 0))],
        out_specs=pl.BlockSpec((1, L, VW), lambda b, c: (b, c, 0)),
        out_shape=jax.ShapeDtypeStruct((bsz, t, VW), BF16),
        scratch_shapes=[pltpu.VMEM((L + CONV_HALO, 2 * QK), F32),
                        pltpu.VMEM((HEADS, DK, DV), F32),
                        pltpu.VMEM((8, DK), F32),
                        pltpu.VMEM((8, 128), F32)],
        compiler_params=_cparams(("parallel", "arbitrary")),
        name="mlstm",
    )(proj, proj, proj, small, small_t, conv_w, bias_col, bias_row)


def _gla_kernel(qk_ref, v_ref, og_ref, gc_ref, wg_ref, bg_ref, gn_ref, o_ref, cum_scr, s_st):
    L = GLA_CHUNK

    @pl.when(pl.program_id(1) == 0)
    def _():
        s_st[...] = jnp.zeros_like(s_st)

    z = _dot(gc_ref[0].astype(BF16), wg_ref[...]) + bg_ref[...]
    log_a = _log_sigmoid(z) * (1.0 / GATE_TAU)
    cum = _tri_left(_tri(L, True), log_a)
    cum_scr[...] = cum

    rows = lax.broadcasted_iota(jnp.int32, (L, 1), 0)
    eye = (lax.broadcasted_iota(jnp.int32, (DK, DK), 0)
           == lax.broadcasted_iota(jnp.int32, (DK, DK), 1))

    for h in range(HEADS):
        q_f = qk_ref[0, :, h * DK:(h + 1) * DK] * (DK ** -0.5)
        k_f = qk_ref[0, :, QK + h * DK:QK + (h + 1) * DK]
        v_f = v_ref[0, :, h * DV:(h + 1) * DV]
        cum_h = cum[:, h * DK:(h + 1) * DK]

        def intra(g, o, h=h, q_f=q_f, cum_h=cum_h):
            base = pl.multiple_of(g * 8, 8)
            k_g = qk_ref[0, pl.ds(base, 8), QK + h * DK:QK + (h + 1) * DK]
            c_g = cum_scr[pl.ds(base, 8), h * DK:(h + 1) * DK]
            v_g = v_ref[0, pl.ds(base, 8), h * DV:(h + 1) * DV]
            for j in range(8):
                x = q_f * k_g[j:j + 1, :] * jnp.exp(jnp.minimum(cum_h - c_g[j:j + 1, :], 0.0))
                a = jnp.sum(x, axis=1, keepdims=True)
                a = jnp.where(rows >= base + j, a, 0.0)
                o = o + a * v_g[j:j + 1, :]
            return o

        o = lax.fori_loop(0, L // 8, intra, jnp.zeros((L, DV), F32))

        st = s_st[h]
        o = o + _dot((q_f * jnp.exp(cum_h)).astype(BF16), st.astype(BF16))
        last = cum_h[L - 1:L, :]
        k_dec = (k_f * jnp.exp(last - cum_h)).astype(BF16)
        kv = lax.dot_general(k_dec, v_f.astype(BF16), (((0,), (0,)), ((), ())),
                             preferred_element_type=F32)
        decay_col = jnp.sum(jnp.where(eye, jnp.exp(last), 0.0), axis=1, keepdims=True)
        s_st[h] = decay_col * st + kv

        o = o * lax.rsqrt(jnp.mean(o * o, axis=1, keepdims=True) + EPS)
        og = og_ref[0, :, h * DV:(h + 1) * DV]
        y = o * gn_ref[:, h * DV:(h + 1) * DV] * (og * _sigmoid(og))
        o_ref[0, :, h * DV:(h + 1) * DV] = y.astype(o_ref.dtype)


def gla_mixer(proj, small, wg_pad, bg, gn, qk_col, v_col, og_col):
    bsz, t, _ = proj.shape
    L = GLA_CHUNK
    assert t % L == 0
    qk_blk, v_blk, og_blk = qk_col // (2 * QK), v_col // VW, og_col // VW
    return pl.pallas_call(
        _gla_kernel,
        grid=(bsz, t // L),
        in_specs=[pl.BlockSpec((1, L, 2 * QK), lambda b, c: (b, c, qk_blk)),
                  pl.BlockSpec((1, L, VW), lambda b, c: (b, c, v_blk)),
                  pl.BlockSpec((1, L, VW), lambda b, c: (b, c, og_blk)),
                  pl.BlockSpec((1, L, SMALL_W), lambda b, c: (b, c, 0)),
                  pl.BlockSpec((SMALL_W, QK), lambda b, c: (0, 0)),
                  pl.BlockSpec((1, QK), lambda b, c: (0, 0)),
                  pl.BlockSpec((1, VW), lambda b, c: (0, 0))],
        out_specs=pl.BlockSpec((1, L, VW), lambda b, c: (b, c, 0)),
        out_shape=jax.ShapeDtypeStruct((bsz, t, VW), BF16),
        scratch_shapes=[pltpu.VMEM((L, QK), F32),
                        pltpu.VMEM((HEADS, DK, DV), F32)],
        compiler_params=_cparams(("parallel", "arbitrary")),
        name="gla",
    )(proj, proj, proj, small, wg_pad, bg, gn)


def _projection_layout(d_model):
    a_w = 2 * QK + 2 * VW
    src_a = 0
    src_if = a_w
    src_b = src_if + 2 * HEADS
    src_z = src_b + a_w
    src_g = src_z + GATE_RANK
    end = src_g + 2 * d_model
    main = [(src_a, src_if), (src_b, src_z), (src_g, end)]
    small = [(src_if, src_b), (src_z, src_g)]
    cols = dict(a_qk=0, a_v=2 * QK, a_og=2 * QK + VW, b_qk=a_w, b_v=a_w + 2 * QK,
                b_og=a_w + 2 * QK + VW, ga=2 * a_w, gb=2 * a_w + d_model)
    return main, small, cols, end


def _layer(h, bsz, t_pad, p, tiles):
    m, d = h.shape
    main, small, cols, p_in = _projection_layout(d)
    assert p['w_in'].shape[1] == p_in
    w_main = jnp.concatenate([p['w_in'][:, a:b] for a, b in main], axis=1).astype(BF16)
    w_small = jnp.concatenate([p['w_in'][:, a:b] for a, b in small], axis=1)
    n_small = w_small.shape[1]
    w_small = jnp.pad(w_small, ((0, 0), (0, SMALL_W - n_small))).astype(BF16)

    hn = rmsnorm(h, p['norm_mix'], BF16)
    proj = _matmul(_mm_kernel, hn, w_main, F32, tiles['tm'], tiles['tn_in'], name="in_proj")
    small_p = _matmul(_mm_kernel, hn, w_small, F32, tiles['tm'], SMALL_W, name="in_proj_small")

    proj3 = proj.reshape(bsz, t_pad, proj.shape[1])
    small3 = small_p.reshape(bsz, t_pad, SMALL_W)
    small_t = jnp.swapaxes(small3[:, :, :2 * HEADS], 1, 2)
    bias_col = jnp.pad(p['b_if'], (0, SMALL_W - 2 * HEADS)).reshape(1, SMALL_W)
    bias_row = jnp.broadcast_to(p['b_if'].reshape(2 * HEADS, 1), (2 * HEADS, MLSTM_CHUNK))
    ya = mlstm_mixer(proj3, small3, small_t, p['conv_qk'], bias_col, bias_row,
                     cols['a_qk'], cols['a_v'], cols['a_og'])

    wg_pad = jnp.zeros((SMALL_W, QK), F32).at[2 * HEADS:2 * HEADS + GATE_RANK].set(p['w_gla_gate'])
    yb = gla_mixer(proj3, small3, wg_pad.astype(BF16), p['b_gla_gate'].reshape(1, QK),
                   p['norm_gla'].reshape(1, VW), cols['b_qk'], cols['b_v'], cols['b_og'])

    merged = merge(ya.reshape(m, VW), yb.reshape(m, VW), p['w_br_a'].astype(BF16),
                   p['w_br_b'].astype(BF16), proj, cols['ga'], cols['gb'],
                   tiles['tm'], tiles['tn'])
    h = _matmul(_mm_residual_kernel, merged, p['w_out'].astype(BF16), F32,
                tiles['tm'], tiles['tn'], residual=h, name="out_proj")

    hn = rmsnorm(h, p['norm_mlp'], BF16)
    u = _matmul(_mm_relu2_kernel, hn, p['w_up'].astype(BF16), BF16,
                tiles['tm'], tiles['tn'], name="up_proj")
    return down_proj(u, p['w_down'].astype(BF16), h, tiles['tm'], tiles['tn_down'], tiles['tk'])


def kernel(x, meta, norm_mix, w_in, conv_qk, b_if, w_gla_gate, b_gla_gate, norm_gla, w_br_a,
           w_br_b, w_out, norm_mlp, w_up, w_down, norm_final):
    bsz, seq, d = x.shape
    n = N_META + seq
    t_pad = -(-n // T_ALIGN) * T_ALIGN
    h = jnp.concatenate([jnp.broadcast_to(meta.astype(x.dtype)[None], (bsz, N_META, d)), x,
                         jnp.zeros((bsz, t_pad - n, d), x.dtype)], axis=1)
    h = h.reshape(bsz * t_pad, d)
    tiles = dict(tm=1280, tn=512, tn_in=512, tn_down=1024, tk=2048)
    for l in range(w_in.shape[0]):
        p = dict(norm_mix=norm_mix[l], w_in=w_in[l], conv_qk=conv_qk[l], b_if=b_if[l],
                 w_gla_gate=w_gla_gate[l], b_gla_gate=b_gla_gate[l], norm_gla=norm_gla[l],
                 w_br_a=w_br_a[l], w_br_b=w_br_b[l], w_out=w_out[l], norm_mlp=norm_mlp[l],
                 w_up=w_up[l], w_down=w_down[l])
        h = _layer(h, bsz, t_pad, p, tiles)
    out = rmsnorm(h, norm_final, x.dtype)
    return out.reshape(bsz, t_pad, d)[:, N_META:N_META + seq]
```

```python
import functools

import jax
import jax.numpy as jnp
from jax import lax
from jax.experimental import pallas as pl
from jax.experimental.pallas import tpu as pltpu

F32 = jnp.float32
BF16 = jnp.bfloat16

N_META = 16
EPS = 1e-6
HEADS = 4
DK = 128
DV = 256
QK = HEADS * DK
VW = HEADS * DV
CONV_W = 4
GATE_RANK = 16
GATE_TAU = 16.0
SMALL_W = 128
CONV_HALO = 8

MLSTM_CHUNK = 128
GLA_CHUNK = 64
T_ALIGN = 128

VMEM_LIMIT_BYTES = 56 * 1024 * 1024


def _cparams(sem):
    return pltpu.CompilerParams(dimension_semantics=sem, vmem_limit_bytes=VMEM_LIMIT_BYTES)


def _tile(n, pref):
    if n <= pref:
        return n
    t = (pref // 128) * 128
    while t >= 128:
        if n % t == 0:
            return t
        t -= 128
    raise ValueError(f"no 128-aligned tile for {n}")


def _sigmoid(x):
    return 1.0 / (1.0 + jnp.exp(-x))


def _log_sigmoid(x):
    return jnp.minimum(x, 0.0) - jnp.log1p(jnp.exp(-jnp.abs(x)))


def _split3(x):
    hi = x.astype(BF16)
    r = x - hi.astype(F32)
    mid = r.astype(BF16)
    lo = (r - mid.astype(F32)).astype(BF16)
    return hi, mid, lo


def _dot(a, b):
    return jnp.dot(a, b, preferred_element_type=F32)


def _tri_left(tri, x):
    hi, mid, lo = _split3(x)
    return (_dot(tri, lo) + _dot(tri, mid)) + _dot(tri, hi)


def _tri_right(x, tri):
    hi, mid, lo = _split3(x)
    return (_dot(lo, tri) + _dot(mid, tri)) + _dot(hi, tri)


def _tri(n, lower):
    r = lax.broadcasted_iota(jnp.int32, (n, n), 0)
    c = lax.broadcasted_iota(jnp.int32, (n, n), 1)
    keep = (r >= c) if lower else (r <= c)
    return jnp.where(keep, 1.0, 0.0).astype(BF16)


def _rmsnorm_kernel(x_ref, g_ref, o_ref):
    x = x_ref[...]
    ms = jnp.mean(x * x, axis=-1, keepdims=True)
    o_ref[...] = (x * lax.rsqrt(ms + EPS) * g_ref[...]).astype(o_ref.dtype)


def rmsnorm(x, g, out_dtype, rows=256):
    m, d = x.shape
    tr = _tile(m, rows)
    return pl.pallas_call(
        _rmsnorm_kernel,
        grid=(m // tr,),
        in_specs=[pl.BlockSpec((tr, d), lambda i: (i, 0)),
                  pl.BlockSpec((1, d), lambda i: (0, 0))],
        out_specs=pl.BlockSpec((tr, d), lambda i: (i, 0)),
        out_shape=jax.ShapeDtypeStruct((m, d), out_dtype),
        compiler_params=_cparams(("parallel",)),
        name="rmsnorm",
    )(x, g.reshape(1, d).astype(F32))


def _mm_kernel(x_ref, w_ref, o_ref):
    o_ref[...] = _dot(x_ref[...], w_ref[...]).astype(o_ref.dtype)


def _mm_relu2_kernel(x_ref, w_ref, o_ref):
    u = jnp.maximum(_dot(x_ref[...], w_ref[...]), 0.0)
    o_ref[...] = (u * u).astype(o_ref.dtype)


def _mm_residual_kernel(x_ref, w_ref, h_ref, o_ref):
    o_ref[...] = h_ref[...] + _dot(x_ref[...], w_ref[...])


def _matmul(body, x, w, layer, out_dtype, tm, tn, residual=None, name="matmul"):
    m, k = x.shape
    n = w.shape[2]
    tm, tn = _tile(m, tm), _tile(n, tn)
    in_specs = [pl.BlockSpec((tm, k), lambda i, j: (i, 0)),
                pl.BlockSpec((None, k, tn), lambda i, j: (layer, 0, j))]
    args = [x, w]
    if residual is not None:
        in_specs.append(pl.BlockSpec((tm, tn), lambda i, j: (i, j)))
        args.append(residual)
    return pl.pallas_call(
        body,
        grid=(m // tm, n // tn),
        in_specs=in_specs,
        out_specs=pl.BlockSpec((tm, tn), lambda i, j: (i, j)),
        out_shape=jax.ShapeDtypeStruct((m, n), out_dtype),
        compiler_params=_cparams(("parallel", "parallel")),
        name=name,
    )(*args)


def _down_kernel(u_ref, w_ref, h_ref, o_ref, acc_ref):
    kk = pl.program_id(2)

    @pl.when(kk == 0)
    def _():
        acc_ref[...] = h_ref[...]

    acc_ref[...] += _dot(u_ref[...], w_ref[...])

    @pl.when(kk == pl.num_programs(2) - 1)
    def _():
        o_ref[...] = acc_ref[...]


def down_proj(u, w, layer, h, tm, tn, tk):
    m, k = u.shape
    n = w.shape[2]
    tm, tn, tk = _tile(m, tm), _tile(n, tn), _tile(k, tk)
    return pl.pallas_call(
        _down_kernel,
        grid=(m // tm, n // tn, k // tk),
        in_specs=[pl.BlockSpec((tm, tk), lambda i, j, kk: (i, kk)),
                  pl.BlockSpec((None, tk, tn), lambda i, j, kk: (layer, kk, j)),
                  pl.BlockSpec((tm, tn), lambda i, j, kk: (i, j))],
        out_specs=pl.BlockSpec((tm, tn), lambda i, j, kk: (i, j)),
        out_shape=jax.ShapeDtypeStruct((m, n), F32),
        scratch_shapes=[pltpu.VMEM((tm, tn), F32)],
        compiler_params=_cparams(("parallel", "parallel", "arbitrary")),
        name="down_proj",
    )(u, w, h)


def _merge_kernel(ya_ref, yb_ref, wa_ref, wb_ref, ga_ref, gb_ref, o_ref):
    a = _dot(ya_ref[...], wa_ref[...])
    b = _dot(yb_ref[...], wb_ref[...])
    o_ref[...] = (_sigmoid(ga_ref[...]) * a + _sigmoid(gb_ref[...]) * b).astype(o_ref.dtype)


def merge(ya, yb, wa, wb, layer, proj, ga_col, gb_col, tm, tn):
    m, k = ya.shape
    n = wa.shape[2]
    tm, tn = _tile(m, tm), _tile(n, tn)
    ga_blk, gb_blk = ga_col // tn, gb_col // tn
    assert ga_blk * tn == ga_col and gb_blk * tn == gb_col
    return pl.pallas_call(
        _merge_kernel,
        grid=(m // tm, n // tn),
        in_specs=[pl.BlockSpec((tm, k), lambda i, j: (i, 0)),
                  pl.BlockSpec((tm, k), lambda i, j: (i, 0)),
                  pl.BlockSpec((None, k, tn), lambda i, j: (layer, 0, j)),
                  pl.BlockSpec((None, k, tn), lambda i, j: (layer, 0, j)),
                  pl.BlockSpec((tm, tn), lambda i, j: (i, ga_blk + j)),
                  pl.BlockSpec((tm, tn), lambda i, j: (i, gb_blk + j))],
        out_specs=pl.BlockSpec((tm, tn), lambda i, j: (i, j)),
        out_shape=jax.ShapeDtypeStruct((m, n), BF16),
        compiler_params=_cparams(("parallel", "parallel")),
        name="merge",
    )(ya, yb, wa, wb, proj, proj)


def _mlstm_kernel(qk_ref, v_ref, og_ref, gc_ref, gr_ref, convw_ref, bc_ref, br_ref,
                  o_ref, xbuf, c_st, n_st, m_st):
    L = MLSTM_CHUNK

    @pl.when(pl.program_id(1) == 0)
    def _():
        xbuf[0:CONV_HALO, :] = jnp.zeros((CONV_HALO, 2 * QK), F32)
        c_st[...] = jnp.zeros_like(c_st)
        n_st[...] = jnp.zeros_like(n_st)
        m_st[...] = jnp.zeros_like(m_st)

    xbuf[CONV_HALO:CONV_HALO + L, :] = qk_ref[0]
    w = convw_ref[...]
    acc = w[CONV_W - 1:CONV_W, :] * xbuf[CONV_HALO:CONV_HALO + L, :]
    for j in range(CONV_W - 1):
        back = CONV_W - 1 - j
        acc = acc + w[j:j + 1, :] * xbuf[CONV_HALO - back:CONV_HALO - back + L, :]
    xbuf[0:CONV_HALO, :] = xbuf[L:L + CONV_HALO, :]
    qk = acc * _sigmoid(acc)

    g_col = gc_ref[0] + bc_ref[...]
    g_row = gr_ref[0] + br_ref[...]
    b_col = _tri_left(_tri(L, True), _log_sigmoid(g_col))
    b_row = _tri_right(_log_sigmoid(g_row), _tri(L, False))

    row = lax.broadcasted_iota(jnp.int32, (L, L), 0)
    col = lax.broadcasted_iota(jnp.int32, (L, L), 1)
    causal = row >= col

    for h in range(HEADS):
        q_f = qk[:, h * DK:(h + 1) * DK]
        q = q_f.astype(BF16)
        k_f = qk[:, QK + h * DK:QK + (h + 1) * DK] * (DK ** -0.5)
        k = k_f.astype(BF16)
        v_f = v_ref[0, :, h * DV:(h + 1) * DV]
        v = v_f.astype(BF16)
        li_c = g_col[:, h:h + 1]
        li_r = g_row[h:h + 1, :]
        b_c = b_col[:, HEADS + h:HEADS + h + 1]
        b_r = b_row[HEADS + h:HEADS + h + 1, :]
        m_prev = m_st[h:h + 1, 0:1]
        ct = c_st[h]
        n_row = n_st[h:h + 1, :]

        log_d = jnp.where(causal, b_c - b_r + li_r, -jnp.inf)
        m_t = jnp.maximum(b_c + m_prev, jnp.max(log_d, axis=1, keepdims=True))
        s = lax.dot_general(q, k, (((1,), (1,)), ((), ())), preferred_element_type=F32)
        s = s * jnp.exp(log_d - m_t)
        carry_w = jnp.exp(b_c + m_prev - m_t)
        num = _dot(s.astype(BF16), v) + carry_w * _dot(q, ct.astype(BF16))
        qn = jnp.sum(q_f * n_row, axis=1, keepdims=True)
        den = jnp.sum(s, axis=1, keepdims=True) + carry_w * qn
        hh = num / jnp.maximum(jnp.abs(den), jnp.exp(-m_t))
        og = og_ref[0, :, h * DV:(h + 1) * DV]
        o_ref[0, :, h * DV:(h + 1) * DV] = (hh * _sigmoid(og)).astype(o_ref.dtype)

        g_tot = b_c[L - 1:L, :]
        log_w = g_tot - b_c + li_c
        m_new = jnp.maximum(g_tot + m_prev, jnp.max(log_w, axis=0, keepdims=True))
        wgt = jnp.exp(log_w - m_new)
        decay = jnp.exp(g_tot + m_prev - m_new)
        kv = lax.dot_general(k, (wgt * v_f).astype(BF16), (((0,), (0,)), ((), ())),
                             preferred_element_type=F32)
        c_st[h] = decay * ct + kv
        n_st[h:h + 1, :] = decay * n_row + jnp.sum(wgt * k_f, axis=0, keepdims=True)
        m_st[h:h + 1, :] = jnp.broadcast_to(m_new, (1, m_st.shape[1]))


def mlstm_mixer(proj, small, small_t, conv_w, bias_col, bias_row, qk_col, v_col, og_col):
    bsz, t, _ = proj.shape
    L = MLSTM_CHUNK
    assert t % L == 0
    qk_blk, v_blk, og_blk = qk_col // (2 * QK), v_col // VW, og_col // VW
    return pl.pallas_call(
        _mlstm_kernel,
        grid=(bsz, t // L),
        in_specs=[pl.BlockSpec((1, L, 2 * QK), lambda b, c: (b, c, qk_blk)),
                  pl.BlockSpec((1, L, VW), lambda b, c: (b, c, v_blk)),
                  pl.BlockSpec((1, L, VW), lambda b, c: (b, c, og_blk)),
                  pl.BlockSpec((1, L, SMALL_W), lambda b, c: (b, c, 0)),
                  pl.BlockSpec((1, 8, L), lambda b, c: (b, 0, c)),
                  pl.BlockSpec((CONV_W, 2 * QK), lambda b, c: (0, 0)),
                  pl.BlockSpec((1, SMALL_W), lambda b, c: (0, 0)),
                  pl.BlockSpec((8, L), lambda b, c: (0, 0))],
        out_specs=pl.BlockSpec((1, L, VW), lambda b, c: (b, c, 0)),
        out_shape=jax.ShapeDtypeStruct((bsz, t, VW), BF16),
        scratch_shapes=[pltpu.VMEM((L + CONV_HALO, 2 * QK), F32),
                        pltpu.VMEM((HEADS, DK, DV), F32),
                        pltpu.VMEM((8, DK), F32),
                        pltpu.VMEM((8, 128), F32)],
        compiler_params=_cparams(("parallel", "arbitrary")),
        name="mlstm",
    )(proj, proj, proj, small, small_t, conv_w, bias_col, bias_row)


def _half_block_ref(x, w):
    n_rows, n = x.shape
    if 2 * w >= 8:
        x3 = x.reshape(n_rows // (2 * w), 2 * w, n)
        return jnp.broadcast_to(x3[:, w - 1:w, :], x3.shape).reshape(n_rows, n)
    x3 = x.reshape(n_rows // 8, 8, n)
    sub = lax.broadcasted_iota(jnp.int32, x3.shape, 1)
    groups = 8 // (2 * w)
    pick = lambda g: jnp.broadcast_to(x3[:, g * 2 * w + w - 1:g * 2 * w + w, :], x3.shape)
    out = pick(groups - 1)
    for g in reversed(range(groups - 1)):
        out = jnp.where(sub < (g + 1) * 2 * w, pick(g), out)
    return out.reshape(n_rows, n)


def _gla_kernel(qk_ref, v_ref, og_ref, gc_ref, wg_ref, bg_ref, gn_ref, o_ref, s_st):
    L = GLA_CHUNK

    @pl.when(pl.program_id(1) == 0)
    def _():
        s_st[...] = jnp.zeros_like(s_st)

    z = _dot(gc_ref[0].astype(BF16), wg_ref[...]) + bg_ref[...]
    log_a = _log_sigmoid(z) * (1.0 / GATE_TAU)
    cum = _tri_left(_tri(L, True), log_a)

    row = lax.broadcasted_iota(jnp.int32, (L, L), 0)
    col = lax.broadcasted_iota(jnp.int32, (L, L), 1)
    level = jnp.where(row > col, 31 - lax.clz(row ^ col), -1)
    diag = row == col
    eye = (lax.broadcasted_iota(jnp.int32, (DK, DK), 0)
           == lax.broadcasted_iota(jnp.int32, (DK, DK), 1))
    n_levels = L.bit_length() - 1

    for h in range(HEADS):
        q_f = qk_ref[0, :, h * DK:(h + 1) * DK] * (DK ** -0.5)
        k_f = qk_ref[0, :, QK + h * DK:QK + (h + 1) * DK]
        v = v_ref[0, :, h * DV:(h + 1) * DV].astype(BF16)
        cum_h = cum[:, h * DK:(h + 1) * DK]

        att = jnp.where(diag, jnp.sum(q_f * k_f, axis=1, keepdims=True), 0.0)
        for lw in range(n_levels):
            f = jnp.exp(-jnp.abs(cum_h - _half_block_ref(cum_h, 1 << lw)))
            p = lax.dot_general((q_f * f).astype(BF16), (k_f * f).astype(BF16),
                                (((1,), (1,)), ((), ())), preferred_element_type=F32)
            att = att + jnp.where(level == lw, p, 0.0)

        st = s_st[h]
        o = _dot(att.astype(BF16), v) + _dot((q_f * jnp.exp(cum_h)).astype(BF16), st.astype(BF16))
        last = cum_h[L - 1:L, :]
        k_dec = (k_f * jnp.exp(last - cum_h)).astype(BF16)
        kv = lax.dot_general(k_dec, v, (((0,), (0,)), ((), ())),
                             preferred_element_type=F32)
        decay_col = jnp.sum(jnp.where(eye, jnp.exp(last), 0.0), axis=1, keepdims=True)
        s_st[h] = decay_col * st + kv

        o = o * lax.rsqrt(jnp.mean(o * o, axis=1, keepdims=True) + EPS)
        og = og_ref[0, :, h * DV:(h + 1) * DV]
        y = o * gn_ref[:, h * DV:(h + 1) * DV] * (og * _sigmoid(og))
        o_ref[0, :, h * DV:(h + 1) * DV] = y.astype(o_ref.dtype)


def gla_mixer(proj, small, wg_pad, bg, gn, qk_col, v_col, og_col):
    bsz, t, _ = proj.shape
    L = GLA_CHUNK
    assert t % L == 0
    qk_blk, v_blk, og_blk = qk_col // (2 * QK), v_col // VW, og_col // VW
    return pl.pallas_call(
        _gla_kernel,
        grid=(bsz, t // L),
        in_specs=[pl.BlockSpec((1, L, 2 * QK), lambda b, c: (b, c, qk_blk)),
                  pl.BlockSpec((1, L, VW), lambda b, c: (b, c, v_blk)),
                  pl.BlockSpec((1, L, VW), lambda b, c: (b, c, og_blk)),
                  pl.BlockSpec((1, L, SMALL_W), lambda b, c: (b, c, 0)),
                  pl.BlockSpec((SMALL_W, QK), lambda b, c: (0, 0)),
                  pl.BlockSpec((1, QK), lambda b, c: (0, 0)),
                  pl.BlockSpec((1, VW), lambda b, c: (0, 0))],
        out_specs=pl.BlockSpec((1, L, VW), lambda b, c: (b, c, 0)),
        out_shape=jax.ShapeDtypeStruct((bsz, t, VW), BF16),
        scratch_shapes=[pltpu.VMEM((HEADS, DK, DV), F32)],
        compiler_params=_cparams(("parallel", "arbitrary")),
        name="gla",
    )(proj, proj, proj, small, wg_pad, bg, gn)


def _projection_layout(d_model):
    a_w = 2 * QK + 2 * VW
    src_a = 0
    src_if = a_w
    src_b = src_if + 2 * HEADS
    src_z = src_b + a_w
    src_g = src_z + GATE_RANK
    end = src_g + 2 * d_model
    main = [(src_a, src_if), (src_b, src_z), (src_g, end)]
    small = [(src_if, src_b), (src_z, src_g)]
    cols = dict(a_qk=0, a_v=2 * QK, a_og=2 * QK + VW, b_qk=a_w, b_v=a_w + 2 * QK,
                b_og=a_w + 2 * QK + VW, ga=2 * a_w, gb=2 * a_w + d_model)
    return main, small, cols, end


def _layer(h, bsz, t_pad, l, p, w, cols, tiles):
    m, d = h.shape
    hn = rmsnorm(h, p['norm_mix'], BF16)
    proj = _matmul(_mm_kernel, hn, w['in_main'], l, F32, tiles['tm'], tiles['tn_in'], name="in_proj")
    small_p = _matmul(_mm_kernel, hn, w['in_small'], l, F32, tiles['tm'], SMALL_W,
                      name="in_proj_small")

    proj3 = proj.reshape(bsz, t_pad, proj.shape[1])
    small3 = small_p.reshape(bsz, t_pad, SMALL_W)
    small_t = jnp.swapaxes(small3[:, :, :2 * HEADS], 1, 2)
    bias_col = jnp.pad(p['b_if'], (0, SMALL_W - 2 * HEADS)).reshape(1, SMALL_W)
    bias_row = jnp.broadcast_to(p['b_if'].reshape(2 * HEADS, 1), (2 * HEADS, MLSTM_CHUNK))
    ya = mlstm_mixer(proj3, small3, small_t, p['conv_qk'], bias_col, bias_row,
                     cols['a_qk'], cols['a_v'], cols['a_og'])

    wg_pad = jnp.zeros((SMALL_W, QK), F32).at[2 * HEADS:2 * HEADS + GATE_RANK].set(p['w_gla_gate'])
    yb = gla_mixer(proj3, small3, wg_pad.astype(BF16), p['b_gla_gate'].reshape(1, QK),
                   p['norm_gla'].reshape(1, VW), cols['b_qk'], cols['b_v'], cols['b_og'])

    merged = merge(ya.reshape(m, VW), yb.reshape(m, VW), w['br_a'], w['br_b'], l, proj,
                   cols['ga'], cols['gb'], tiles['tm'], tiles['tn_merge'])
    h = _matmul(_mm_residual_kernel, merged, w['out'], l, F32,
                tiles['tm'], tiles['tn'], residual=h, name="out_proj")

    hn = rmsnorm(h, p['norm_mlp'], BF16)
    u = _matmul(_mm_relu2_kernel, hn, w['up'], l, BF16, tiles['tm'], tiles['tn'], name="up_proj")
    return down_proj(u, w['down'], l, h, tiles['tm'], tiles['tn_down'], tiles['tk'])


def kernel(x, meta, norm_mix, w_in, conv_qk, b_if, w_gla_gate, b_gla_gate, norm_gla, w_br_a,
           w_br_b, w_out, norm_mlp, w_up, w_down, norm_final):
    bsz, seq, d = x.shape
    n = N_META + seq
    t_pad = -(-n // T_ALIGN) * T_ALIGN
    h = jnp.concatenate([jnp.broadcast_to(meta.astype(x.dtype)[None], (bsz, N_META, d)), x,
                         jnp.zeros((bsz, t_pad - n, d), x.dtype)], axis=1)
    h = h.reshape(bsz * t_pad, d)

    main, small, cols, p_in = _projection_layout(d)
    assert w_in.shape[2] == p_in
    in_small = jnp.concatenate([w_in[:, :, a:b] for a, b in small], axis=2)
    in_small = jnp.pad(in_small, ((0, 0), (0, 0), (0, SMALL_W - in_small.shape[2])))
    w = dict(in_main=jnp.concatenate([w_in[:, :, a:b] for a, b in main], axis=2).astype(BF16),
             in_small=in_small.astype(BF16), br_a=w_br_a.astype(BF16), br_b=w_br_b.astype(BF16),
             out=w_out.astype(BF16), up=w_up.astype(BF16), down=w_down.astype(BF16))

    tiles = dict(tm=1280, tn=512, tn_in=512, tn_merge=1024, tn_down=1024, tk=2048)
    for l in range(w_in.shape[0]):
        p = dict(norm_mix=norm_mix[l], conv_qk=conv_qk[l], b_if=b_if[l],
                 w_gla_gate=w_gla_gate[l], b_gla_gate=b_gla_gate[l], norm_gla=norm_gla[l],
                 norm_mlp=norm_mlp[l])
        h = _layer(h, bsz, t_pad, l, p, w, cols, tiles)
    out = rmsnorm(h, norm_final, x.dtype)
    return out.reshape(bsz, t_pad, d)[:, N_META:N_META + seq]
```

```python
import functools

import jax
import jax.numpy as jnp
from jax import lax
from jax.experimental import pallas as pl
from jax.experimental.pallas import tpu as pltpu

F32 = jnp.float32
BF16 = jnp.bfloat16

N_META = 16
EPS = 1e-6
HEADS = 4
DK = 128
DV = 256
QK = HEADS * DK
VW = HEADS * DV
CONV_W = 4
GATE_RANK = 16
GATE_TAU = 16.0
SMALL_W = 128
CONV_HALO = 8

MLSTM_CHUNK = 128
GLA_CHUNK = 64
T_ALIGN = 128

VMEM_LIMIT_BYTES = 56 * 1024 * 1024


def _cparams(sem):
    return pltpu.CompilerParams(dimension_semantics=sem, vmem_limit_bytes=VMEM_LIMIT_BYTES)


def _tile(n, pref):
    if n <= pref:
        return n
    t = (pref // 128) * 128
    while t >= 128:
        if n % t == 0:
            return t
        t -= 128
    raise ValueError(f"no 128-aligned tile for {n}")


def _sigmoid(x):
    return 1.0 / (1.0 + jnp.exp(-x))


def _log_sigmoid(x):
    return jnp.minimum(x, 0.0) - jnp.log1p(jnp.exp(-jnp.abs(x)))


def _split3(x):
    hi = x.astype(BF16)
    r = x - hi.astype(F32)
    mid = r.astype(BF16)
    lo = (r - mid.astype(F32)).astype(BF16)
    return hi, mid, lo


def _dot(a, b):
    return jnp.dot(a, b, preferred_element_type=F32)


def _tri_left(tri, x):
    hi, mid, lo = _split3(x)
    return (_dot(tri, lo) + _dot(tri, mid)) + _dot(tri, hi)


def _tri_right(x, tri):
    hi, mid, lo = _split3(x)
    return (_dot(lo, tri) + _dot(mid, tri)) + _dot(hi, tri)


def _tri(n, lower):
    r = lax.broadcasted_iota(jnp.int32, (n, n), 0)
    c = lax.broadcasted_iota(jnp.int32, (n, n), 1)
    keep = (r >= c) if lower else (r <= c)
    return jnp.where(keep, 1.0, 0.0).astype(BF16)


def _rmsnorm_kernel(x_ref, g_ref, o_ref):
    x = x_ref[...]
    ms = jnp.mean(x * x, axis=-1, keepdims=True)
    o_ref[...] = (x * lax.rsqrt(ms + EPS) * g_ref[...]).astype(o_ref.dtype)


def rmsnorm(x, g, out_dtype, rows=256):
    m, d = x.shape
    tr = _tile(m, rows)
    return pl.pallas_call(
        _rmsnorm_kernel,
        grid=(m // tr,),
        in_specs=[pl.BlockSpec((tr, d), lambda i: (i, 0)),
                  pl.BlockSpec((1, d), lambda i: (0, 0))],
        out_specs=pl.BlockSpec((tr, d), lambda i: (i, 0)),
        out_shape=jax.ShapeDtypeStruct((m, d), out_dtype),
        compiler_params=_cparams(("parallel",)),
        name="rmsnorm",
    )(x, g.reshape(1, d).astype(F32))


def _mm_kernel(x_ref, w_ref, o_ref):
    o_ref[...] = _dot(x_ref[...], w_ref[...]).astype(o_ref.dtype)


def _mm_relu2_kernel(x_ref, w_ref, o_ref):
    u = jnp.maximum(_dot(x_ref[...], w_ref[...]), 0.0)
    o_ref[...] = (u * u).astype(o_ref.dtype)


def _mm_residual_kernel(x_ref, w_ref, h_ref, o_ref):
    o_ref[...] = h_ref[...] + _dot(x_ref[...], w_ref[...])


def _matmul(body, x, w, layer, out_dtype, tm, tn, residual=None, name="matmul"):
    m, k = x.shape
    n = w.shape[2]
    tm, tn = _tile(m, tm), _tile(n, tn)
    in_specs = [pl.BlockSpec((tm, k), lambda i, j: (i, 0)),
                pl.BlockSpec((None, k, tn), lambda i, j: (layer, 0, j))]
    args = [x, w]
    if residual is not None:
        in_specs.append(pl.BlockSpec((tm, tn), lambda i, j: (i, j)))
        args.append(residual)
    return pl.pallas_call(
        body,
        grid=(m // tm, n // tn),
        in_specs=in_specs,
        out_specs=pl.BlockSpec((tm, tn), lambda i, j: (i, j)),
        out_shape=jax.ShapeDtypeStruct((m, n), out_dtype),
        compiler_params=_cparams(("parallel", "parallel")),
        name=name,
    )(*args)


CAST_ROWS = 256


def _up_ws_kernel(x_ref, w_ref, o_ref, wbf_ref):
    @pl.when(pl.program_id(1) == 0)
    def _():
        def cast(r, carry):
            rows = pl.ds(pl.multiple_of(r * CAST_ROWS, CAST_ROWS), CAST_ROWS)
            wbf_ref[rows, :] = w_ref[rows, :].astype(BF16)
            return carry
        lax.fori_loop(0, w_ref.shape[0] // CAST_ROWS, cast, 0)

    u = jnp.maximum(_dot(x_ref[...], wbf_ref[...]), 0.0)
    o_ref[...] = (u * u).astype(o_ref.dtype)


def up_proj_ws(x, w, layer, tm, tn):
    m, k = x.shape
    n = w.shape[2]
    tm, tn = _tile(m, tm), _tile(n, tn)
    return pl.pallas_call(
        _up_ws_kernel,
        grid=(n // tn, m // tm),
        in_specs=[pl.BlockSpec((tm, k), lambda j, i: (i, 0)),
                  pl.BlockSpec((None, k, tn), lambda j, i: (layer, 0, j))],
        out_specs=pl.BlockSpec((tm, tn), lambda j, i: (i, j)),
        out_shape=jax.ShapeDtypeStruct((m, n), BF16),
        scratch_shapes=[pltpu.VMEM((k, tn), BF16)],
        compiler_params=_cparams(("parallel", "arbitrary")),
        name="up_proj",
    )(x, w)


def _in_ws_kernel(x_ref, wa_ref, wb_ref, o_ref, wbf_ref, *, shifts):
    tn = wa_ref.shape[1]
    j = pl.program_id(0)

    @pl.when(pl.program_id(1) == 0)
    def _():
        for idx, (first, shift) in enumerate(shifts):
            last = shifts[idx + 1][0] if idx + 1 < len(shifts) else None
            cond = j >= first if last is None else jnp.logical_and(j >= first, j < last)

            @pl.when(cond)
            def _(shift=shift):
                def cast(r, carry):
                    rows = pl.ds(pl.multiple_of(r * CAST_ROWS, CAST_ROWS), CAST_ROWS)
                    if shift == 0:
                        wbf_ref[rows, :] = wa_ref[rows, :].astype(BF16)
                    else:
                        wide = jnp.concatenate([wa_ref[rows, :], wb_ref[rows, :]], axis=1)
                        wbf_ref[rows, :] = wide[:, shift:shift + tn].astype(BF16)
                    return carry
                lax.fori_loop(0, wa_ref.shape[0] // CAST_ROWS, cast, 0)

    o_ref[...] = _dot(x_ref[...], wbf_ref[...]).astype(o_ref.dtype)


def in_proj_ws(x, w, layer, n_out, shifts, tm, tn):
    m, k = x.shape
    tm = _tile(m, tm)
    assert n_out % tn == 0 and tn % 128 == 0
    return pl.pallas_call(
        functools.partial(_in_ws_kernel, shifts=shifts),
        grid=(n_out // tn, m // tm),
        in_specs=[pl.BlockSpec((tm, k), lambda j, i: (i, 0)),
                  pl.BlockSpec((None, k, tn), lambda j, i: (layer, 0, j)),
                  pl.BlockSpec((None, k, 128), lambda j, i: (layer, 0, (j + 1) * (tn // 128)))],
        out_specs=pl.BlockSpec((tm, tn), lambda j, i: (i, j)),
        out_shape=jax.ShapeDtypeStruct((m, n_out), F32),
        scratch_shapes=[pltpu.VMEM((k, tn), BF16)],
        compiler_params=_cparams(("parallel", "arbitrary")),
        name="in_proj",
    )(x, w, w)


def _down_kernel(u_ref, w_ref, h_ref, o_ref, acc_ref):
    kk = pl.program_id(2)

    @pl.when(kk == 0)
    def _():
        acc_ref[...] = h_ref[...]

    acc_ref[...] += _dot(u_ref[...], w_ref[...])

    @pl.when(kk == pl.num_programs(2) - 1)
    def _():
        o_ref[...] = acc_ref[...]


def down_proj(u, w, layer, h, tm, tn, tk):
    m, k = u.shape
    n = w.shape[2]
    tm, tn, tk = _tile(m, tm), _tile(n, tn), _tile(k, tk)
    return pl.pallas_call(
        _down_kernel,
        grid=(m // tm, n // tn, k // tk),
        in_specs=[pl.BlockSpec((tm, tk), lambda i, j, kk: (i, kk)),
                  pl.BlockSpec((None, tk, tn), lambda i, j, kk: (layer, kk, j)),
                  pl.BlockSpec((tm, tn), lambda i, j, kk: (i, j))],
        out_specs=pl.BlockSpec((tm, tn), lambda i, j, kk: (i, j)),
        out_shape=jax.ShapeDtypeStruct((m, n), F32),
        scratch_shapes=[pltpu.VMEM((tm, tn), F32)],
        compiler_params=_cparams(("parallel", "parallel", "arbitrary")),
        name="down_proj",
    )(u, w, h)


def _merge_kernel(ya_ref, yb_ref, wa_ref, wb_ref, ga_ref, gb_ref, o_ref):
    a = _dot(ya_ref[...], wa_ref[...])
    b = _dot(yb_ref[...], wb_ref[...])
    o_ref[...] = (_sigmoid(ga_ref[...]) * a + _sigmoid(gb_ref[...]) * b).astype(o_ref.dtype)


def merge(ya, yb, wa, wb, layer, proj, ga_col, gb_col, tm, tn):
    m, k = ya.shape
    n = wa.shape[2]
    tm, tn = _tile(m, tm), _tile(n, tn)
    ga_blk, gb_blk = ga_col // tn, gb_col // tn
    assert ga_blk * tn == ga_col and gb_blk * tn == gb_col
    return pl.pallas_call(
        _merge_kernel,
        grid=(m // tm, n // tn),
        in_specs=[pl.BlockSpec((tm, k), lambda i, j: (i, 0)),
                  pl.BlockSpec((tm, k), lambda i, j: (i, 0)),
                  pl.BlockSpec((None, k, tn), lambda i, j: (layer, 0, j)),
                  pl.BlockSpec((None, k, tn), lambda i, j: (layer, 0, j)),
                  pl.BlockSpec((tm, tn), lambda i, j: (i, ga_blk + j)),
                  pl.BlockSpec((tm, tn), lambda i, j: (i, gb_blk + j))],
        out_specs=pl.BlockSpec((tm, tn), lambda i, j: (i, j)),
        out_shape=jax.ShapeDtypeStruct((m, n), BF16),
        compiler_params=_cparams(("parallel", "parallel")),
        name="merge",
    )(ya, yb, wa, wb, proj, proj)


def _mlstm_kernel(qk_ref, v_ref, og_ref, gc_ref, gr_ref, convw_ref, bc_ref, br_ref,
                  o_ref, xbuf, c_st, n_st, m_st):
    L = MLSTM_CHUNK

    @pl.when(pl.program_id(1) == 0)
    def _():
        xbuf[0:CONV_HALO, :] = jnp.zeros((CONV_HALO, 2 * QK), F32)
        c_st[...] = jnp.zeros_like(c_st)
        n_st[...] = jnp.zeros_like(n_st)
        m_st[...] = jnp.zeros_like(m_st)

    xbuf[CONV_HALO:CONV_HALO + L, :] = qk_ref[0]
    w = convw_ref[...]
    acc = w[CONV_W - 1:CONV_W, :] * xbuf[CONV_HALO:CONV_HALO + L, :]
    for j in range(CONV_W - 1):
        back = CONV_W - 1 - j
        acc = acc + w[j:j + 1, :] * xbuf[CONV_HALO - back:CONV_HALO - back + L, :]
    xbuf[0:CONV_HALO, :] = xbuf[L:L + CONV_HALO, :]
    qk = acc * _sigmoid(acc)

    g_col = gc_ref[0] + bc_ref[...]
    g_row = gr_ref[0] + br_ref[...]
    b_col = _tri_left(_tri(L, True), _log_sigmoid(g_col))
    b_row = _tri_right(_log_sigmoid(g_row), _tri(L, False))

    row = lax.broadcasted_iota(jnp.int32, (L, L), 0)
    col = lax.broadcasted_iota(jnp.int32, (L, L), 1)
    causal = row >= col

    for h in range(HEADS):
        q_f = qk[:, h * DK:(h + 1) * DK]
        q = q_f.astype(BF16)
        k_f = qk[:, QK + h * DK:QK + (h + 1) * DK] * (DK ** -0.5)
        k = k_f.astype(BF16)
        v_f = v_ref[0, :, h * DV:(h + 1) * DV]
        v = v_f.astype(BF16)
        li_c = g_col[:, h:h + 1]
        li_r = g_row[h:h + 1, :]
        b_c = b_col[:, HEADS + h:HEADS + h + 1]
        b_r = b_row[HEADS + h:HEADS + h + 1, :]
        m_prev = m_st[h:h + 1, 0:1]
        ct = c_st[h]
        n_row = n_st[h:h + 1, :]

        log_d = jnp.where(causal, b_c - b_r + li_r, -jnp.inf)
        m_t = jnp.maximum(b_c + m_prev, jnp.max(log_d, axis=1, keepdims=True))
        s = lax.dot_general(q, k, (((1,), (1,)), ((), ())), preferred_element_type=F32)
        s = s * jnp.exp(log_d - m_t)
        carry_w = jnp.exp(b_c + m_prev - m_t)
        num = _dot(s.astype(BF16), v) + carry_w * _dot(q, ct.astype(BF16))
        qn = jnp.sum(q_f * n_row, axis=1, keepdims=True)
        den = jnp.sum(s, axis=1, keepdims=True) + carry_w * qn
        hh = num / jnp.maximum(jnp.abs(den), jnp.exp(-m_t))
        og = og_ref[0, :, h * DV:(h + 1) * DV]
        o_ref[0, :, h * DV:(h + 1) * DV] = (hh * _sigmoid(og)).astype(o_ref.dtype)

        g_tot = b_c[L - 1:L, :]
        log_w = g_tot - b_c + li_c
        m_new = jnp.maximum(g_tot + m_prev, jnp.max(log_w, axis=0, keepdims=True))
        wgt = jnp.exp(log_w - m_new)
        decay = jnp.exp(g_tot + m_prev - m_new)
        kv = lax.dot_general(k, (wgt * v_f).astype(BF16), (((0,), (0,)), ((), ())),
                             preferred_element_type=F32)
        c_st[h] = decay * ct + kv
        n_st[h:h + 1, :] = decay * n_row + jnp.sum(wgt * k_f, axis=0, keepdims=True)
        m_st[h:h + 1, :] = jnp.broadcast_to(m_new, (1, m_st.shape[1]))


def mlstm_mixer(proj, small, small_t, conv_w, bias_col, bias_row, qk_col, v_col, og_col):
    bsz, t, _ = proj.shape
    L = MLSTM_CHUNK
    assert t % L == 0
    qk_blk, v_blk, og_blk = qk_col // (2 * QK), v_col // VW, og_col // VW
    return pl.pallas_call(
        _mlstm_kernel,
        grid=(bsz, t // L),
        in_specs=[pl.BlockSpec((1, L, 2 * QK), lambda b, c: (b, c, qk_blk)),
                  pl.BlockSpec((1, L, VW), lambda b, c: (b, c, v_blk)),
                  pl.BlockSpec((1, L, VW), lambda b, c: (b, c, og_blk)),
                  pl.BlockSpec((1, L, SMALL_W), lambda b, c: (b, c, 0)),
                  pl.BlockSpec((1, 8, L), lambda b, c: (b, 0, c)),
                  pl.BlockSpec((CONV_W, 2 * QK), lambda b, c: (0, 0)),
                  pl.BlockSpec((1, SMALL_W), lambda b, c: (0, 0)),
                  pl.BlockSpec((8, L), lambda b, c: (0, 0))],
        out_specs=pl.BlockSpec((1, L, VW), lambda b, c: (b, c, 0)),
        out_shape=jax.ShapeDtypeStruct((bsz, t, VW), BF16),
        scratch_shapes=[pltpu.VMEM((L + CONV_HALO, 2 * QK), F32),
                        pltpu.VMEM((HEADS, DK, DV), F32),
                        pltpu.VMEM((8, DK), F32),
                        pltpu.VMEM((8, 128), F32)],
        compiler_params=_cparams(("parallel", "arbitrary")),
        name="mlstm",
    )(proj, proj, proj, small, small_t, conv_w, bias_col, bias_row)


def _half_block_ref(x, w):
    n_rows, n = x.shape
    if 2 * w >= 8:
        x3 = x.reshape(n_rows // (2 * w), 2 * w, n)
        return jnp.broadcast_to(x3[:, w - 1:w, :], x3.shape).reshape(n_rows, n)
    x3 = x.reshape(n_rows // 8, 8, n)
    sub = lax.broadcasted_iota(jnp.int32, x3.shape, 1)
    groups = 8 // (2 * w)
    pick = lambda g: jnp.broadcast_to(x3[:, g * 2 * w + w - 1:g * 2 * w + w, :], x3.shape)
    out = pick(groups - 1)
    for g in reversed(range(groups - 1)):
        out = jnp.where(sub < (g + 1) * 2 * w, pick(g), out)
    return out.reshape(n_rows, n)


def _gla_kernel(qk_ref, v_ref, og_ref, gc_ref, wg_ref, bg_ref, gn_ref, o_ref, s_st):
    L = GLA_CHUNK

    @pl.when(pl.program_id(1) == 0)
    def _():
        s_st[...] = jnp.zeros_like(s_st)

    z = _dot(gc_ref[0].astype(BF16), wg_ref[...]) + bg_ref[...]
    log_a = _log_sigmoid(z) * (1.0 / GATE_TAU)
    cum = _tri_left(_tri(L, True), log_a)

    row = lax.broadcasted_iota(jnp.int32, (L, L), 0)
    col = lax.broadcasted_iota(jnp.int32, (L, L), 1)
    level = jnp.where(row > col, 31 - lax.clz(row ^ col), -1)
    diag = row == col
    eye = (lax.broadcasted_iota(jnp.int32, (DK, DK), 0)
           == lax.broadcasted_iota(jnp.int32, (DK, DK), 1))
    n_levels = L.bit_length() - 1

    for h in range(HEADS):
        q_f = qk_ref[0, :, h * DK:(h + 1) * DK] * (DK ** -0.5)
        k_f = qk_ref[0, :, QK + h * DK:QK + (h + 1) * DK]
        v = v_ref[0, :, h * DV:(h + 1) * DV].astype(BF16)
        cum_h = cum[:, h * DK:(h + 1) * DK]

        att = jnp.where(diag, jnp.sum(q_f * k_f, axis=1, keepdims=True), 0.0)
        for lw in range(n_levels):
            f = jnp.exp(-jnp.abs(cum_h - _half_block_ref(cum_h, 1 << lw)))
            p = lax.dot_general((q_f * f).astype(BF16), (k_f * f).astype(BF16),
                                (((1,), (1,)), ((), ())), preferred_element_type=F32)
            att = att + jnp.where(level == lw, p, 0.0)

        st = s_st[h]
        o = _dot(att.astype(BF16), v) + _dot((q_f * jnp.exp(cum_h)).astype(BF16), st.astype(BF16))
        last = cum_h[L - 1:L, :]
        k_dec = (k_f * jnp.exp(last - cum_h)).astype(BF16)
        kv = lax.dot_general(k_dec, v, (((0,), (0,)), ((), ())),
                             preferred_element_type=F32)
        decay_col = jnp.sum(jnp.where(eye, jnp.exp(last), 0.0), axis=1, keepdims=True)
        s_st[h] = decay_col * st + kv

        o = o * lax.rsqrt(jnp.mean(o * o, axis=1, keepdims=True) + EPS)
        og = og_ref[0, :, h * DV:(h + 1) * DV]
        y = o * gn_ref[:, h * DV:(h + 1) * DV] * (og * _sigmoid(og))
        o_ref[0, :, h * DV:(h + 1) * DV] = y.astype(o_ref.dtype)


def gla_mixer(proj, small, wg_pad, bg, gn, qk_col, v_col, og_col):
    bsz, t, _ = proj.shape
    L = GLA_CHUNK
    assert t % L == 0
    qk_blk, v_blk, og_blk = qk_col // (2 * QK), v_col // VW, og_col // VW
    return pl.pallas_call(
        _gla_kernel,
        grid=(bsz, t // L),
        in_specs=[pl.BlockSpec((1, L, 2 * QK), lambda b, c: (b, c, qk_blk)),
                  pl.BlockSpec((1, L, VW), lambda b, c: (b, c, v_blk)),
                  pl.BlockSpec((1, L, VW), lambda b, c: (b, c, og_blk)),
                  pl.BlockSpec((1, L, SMALL_W), lambda b, c: (b, c, 0)),
                  pl.BlockSpec((SMALL_W, QK), lambda b, c: (0, 0)),
                  pl.BlockSpec((1, QK), lambda b, c: (0, 0)),
                  pl.BlockSpec((1, VW), lambda b, c: (0, 0))],
        out_specs=pl.BlockSpec((1, L, VW), lambda b, c: (b, c, 0)),
        out_shape=jax.ShapeDtypeStruct((bsz, t, VW), BF16),
        scratch_shapes=[pltpu.VMEM((HEADS, DK, DV), F32)],
        compiler_params=_cparams(("parallel", "arbitrary")),
        name="gla",
    )(proj, proj, proj, small, wg_pad, bg, gn)


def _projection_layout(d_model):
    a_w = 2 * QK + 2 * VW
    src_a = 0
    src_if = a_w
    src_b = src_if + 2 * HEADS
    src_z = src_b + a_w
    src_g = src_z + GATE_RANK
    end = src_g + 2 * d_model
    main = [(src_a, src_if), (src_b, src_z), (src_g, end)]
    small = [(src_if, src_b), (src_z, src_g)]
    def shifts(tn):
        out, start = [], 0
        for a, b in main:
            assert start % tn == 0 and 0 <= a - start < 128
            out.append((start // tn, a - start))
            start += b - a
        return tuple(out)

    cols = dict(a_qk=0, a_v=2 * QK, a_og=2 * QK + VW, b_qk=a_w, b_v=a_w + 2 * QK,
                b_og=a_w + 2 * QK + VW, ga=2 * a_w, gb=2 * a_w + d_model,
                n_main=2 * a_w + 2 * d_model, shifts=shifts)
    return main, small, cols, end


def _layer(h, bsz, t_pad, l, p, w, cols, tiles):
    m, d = h.shape
    hn = rmsnorm(h, p['norm_mix'], BF16)
    proj = in_proj_ws(hn, w['in_f32'], l, cols['n_main'], cols['shifts'](tiles['tn_in']),
                      tiles['tm'], tiles['tn_in'])
    small_p = _matmul(_mm_kernel, hn, w['in_small'], l, F32, tiles['tm'], SMALL_W,
                      name="in_proj_small")

    proj3 = proj.reshape(bsz, t_pad, proj.shape[1])
    small3 = small_p.reshape(bsz, t_pad, SMALL_W)
    small_t = jnp.swapaxes(small3[:, :, :2 * HEADS], 1, 2)
    bias_col = jnp.pad(p['b_if'], (0, SMALL_W - 2 * HEADS)).reshape(1, SMALL_W)
    bias_row = jnp.broadcast_to(p['b_if'].reshape(2 * HEADS, 1), (2 * HEADS, MLSTM_CHUNK))
    ya = mlstm_mixer(proj3, small3, small_t, p['conv_qk'], bias_col, bias_row,
                     cols['a_qk'], cols['a_v'], cols['a_og'])

    wg_pad = jnp.zeros((SMALL_W, QK), F32).at[2 * HEADS:2 * HEADS + GATE_RANK].set(p['w_gla_gate'])
    yb = gla_mixer(proj3, small3, wg_pad.astype(BF16), p['b_gla_gate'].reshape(1, QK),
                   p['norm_gla'].reshape(1, VW), cols['b_qk'], cols['b_v'], cols['b_og'])

    merged = merge(ya.reshape(m, VW), yb.reshape(m, VW), w['br_a'], w['br_b'], l, proj,
                   cols['ga'], cols['gb'], tiles['tm'], tiles['tn_merge'])
    h = _matmul(_mm_residual_kernel, merged, w['out'], l, F32,
                tiles['tm'], tiles['tn'], residual=h, name="out_proj")

    hn = rmsnorm(h, p['norm_mlp'], BF16)
    u = up_proj_ws(hn, w['up_f32'], l, tiles['tm'], tiles['tn'])
    return down_proj(u, w['down'], l, h, tiles['tm'], tiles['tn_down'], tiles['tk'])


def kernel(x, meta, norm_mix, w_in, conv_qk, b_if, w_gla_gate, b_gla_gate, norm_gla, w_br_a,
           w_br_b, w_out, norm_mlp, w_up, w_down, norm_final):
    bsz, seq, d = x.shape
    n = N_META + seq
    t_pad = -(-n // T_ALIGN) * T_ALIGN
    h = jnp.concatenate([jnp.broadcast_to(meta.astype(x.dtype)[None], (bsz, N_META, d)), x,
                         jnp.zeros((bsz, t_pad - n, d), x.dtype)], axis=1)
    h = h.reshape(bsz * t_pad, d)

    main, small, cols, p_in = _projection_layout(d)
    assert w_in.shape[2] == p_in
    in_small = jnp.concatenate([w_in[:, :, a:b] for a, b in small], axis=2)
    in_small = jnp.pad(in_small, ((0, 0), (0, 0), (0, SMALL_W - in_small.shape[2])))
    w = dict(in_f32=w_in, in_small=in_small.astype(BF16), br_a=w_br_a.astype(BF16),
             br_b=w_br_b.astype(BF16), out=w_out.astype(BF16), up_f32=w_up,
             down=w_down.astype(BF16))

    tiles = dict(tm=1280, tn=512, tn_in=512, tn_merge=1024, tn_down=1024, tk=2048)
    for l in range(w_in.shape[0]):
        p = dict(norm_mix=norm_mix[l], conv_qk=conv_qk[l], b_if=b_if[l],
                 w_gla_gate=w_gla_gate[l], b_gla_gate=b_gla_gate[l], norm_gla=norm_gla[l],
                 norm_mlp=norm_mlp[l])
        h = _layer(h, bsz, t_pad, l, p, w, cols, tiles)
    out = rmsnorm(h, norm_final, x.dtype)
    return out.reshape(bsz, t_pad, d)[:, N_META:N_META + seq]
```

```python
import functools

import jax
import jax.numpy as jnp
from jax import lax
from jax.experimental import pallas as pl
from jax.experimental.pallas import tpu as pltpu

F32 = jnp.float32
BF16 = jnp.bfloat16

N_META = 16
EPS = 1e-6
HEADS = 4
DK = 128
DV = 256
QK = HEADS * DK
VW = HEADS * DV
CONV_W = 4
GATE_RANK = 16
GATE_TAU = 16.0
LANES = 128
SMALL_W = LANES
CONV_HALO = 8

MLSTM_CHUNK = 128
GLA_CHUNK = 64
T_ALIGN = 128

VMEM_LIMIT_BYTES = 60 * 1024 * 1024


def _cparams(sem):
    return pltpu.CompilerParams(dimension_semantics=sem, vmem_limit_bytes=VMEM_LIMIT_BYTES)


def _tile(n, pref):
    if n <= pref:
        return n
    t = (pref // LANES) * LANES
    while t >= LANES:
        if n % t == 0:
            return t
        t -= LANES
    raise ValueError(f"no 128-aligned tile for {n}")


def _sigmoid(x):
    return 1.0 / (1.0 + jnp.exp(-x))


def _log_sigmoid(x):
    return jnp.minimum(x, 0.0) - jnp.log1p(jnp.exp(-jnp.abs(x)))


def _split3(x):
    hi = x.astype(BF16)
    r = x - hi.astype(F32)
    mid = r.astype(BF16)
    lo = (r - mid.astype(F32)).astype(BF16)
    return hi, mid, lo


def _dot(a, b):
    return jnp.dot(a, b, preferred_element_type=F32)


def _dot_nt(a, b):
    return lax.dot_general(a, b, (((1,), (1,)), ((), ())), preferred_element_type=F32)


def _dot_tn(a, b):
    return lax.dot_general(a, b, (((0,), (0,)), ((), ())), preferred_element_type=F32)


def _tri_left(tri, x):
    hi, mid, lo = _split3(x)
    return (_dot(tri, lo) + _dot(tri, mid)) + _dot(tri, hi)


def _tri_right(x, tri):
    hi, mid, lo = _split3(x)
    return (_dot(lo, tri) + _dot(mid, tri)) + _dot(hi, tri)


def _tri(n, lower):
    r = lax.broadcasted_iota(jnp.int32, (n, n), 0)
    c = lax.broadcasted_iota(jnp.int32, (n, n), 1)
    keep = (r >= c) if lower else (r <= c)
    return jnp.where(keep, 1.0, 0.0).astype(BF16)


def _fold_lanes(x):
    out = x[:, 0:LANES]
    for c in range(1, x.shape[1] // LANES):
        out = out + x[:, c * LANES:(c + 1) * LANES]
    return out


def _row_scale(ssq, d):
    return lax.rsqrt(jnp.sum(ssq, axis=1, keepdims=True) * (1.0 / d) + EPS)


def _rmsnorm_kernel(x_ref, g_ref, o_ref):
    x = x_ref[...]
    ms = jnp.mean(x * x, axis=-1, keepdims=True)
    o_ref[...] = (x * lax.rsqrt(ms + EPS) * g_ref[...]).astype(o_ref.dtype)


def rmsnorm(x, g, out_dtype, rows=256):
    m, d = x.shape
    tr = _tile(m, rows)
    return pl.pallas_call(
        _rmsnorm_kernel,
        grid=(m // tr,),
        in_specs=[pl.BlockSpec((tr, d), lambda i: (i, 0)),
                  pl.BlockSpec((1, d), lambda i: (0, 0))],
        out_specs=pl.BlockSpec((tr, d), lambda i: (i, 0)),
        out_shape=jax.ShapeDtypeStruct((m, d), out_dtype),
        compiler_params=_cparams(("parallel",)),
        name="rmsnorm",
    )(x, g.reshape(1, d))


def _gain_stats_kernel(x_ref, g_ref, hg_ref, sq_ref):
    x = x_ref[...]
    hg_ref[...] = (x * g_ref[...]).astype(hg_ref.dtype)
    sq_ref[...] = _fold_lanes(x * x)


def gain_stats(x, g, rows=256):
    m, d = x.shape
    tr = _tile(m, rows)
    return pl.pallas_call(
        _gain_stats_kernel,
        grid=(m // tr,),
        in_specs=[pl.BlockSpec((tr, d), lambda i: (i, 0)),
                  pl.BlockSpec((1, d), lambda i: (0, 0))],
        out_specs=[pl.BlockSpec((tr, d), lambda i: (i, 0)),
                   pl.BlockSpec((tr, LANES), lambda i: (i, 0))],
        out_shape=[jax.ShapeDtypeStruct((m, d), BF16),
                   jax.ShapeDtypeStruct((m, LANES), F32)],
        compiler_params=_cparams(("parallel",)),
        name="gain_stats",
    )(x, g.reshape(1, d))


def _normed_kernel(x_ref, w_ref, ssq_ref, o_ref, r_ref, *, w_is_nk, relu2):
    @pl.when(pl.program_id(1) == 0)
    def _():
        r_ref[...] = _row_scale(ssq_ref[...], x_ref.shape[1])

    acc = _dot_nt(x_ref[...], w_ref[...]) if w_is_nk else _dot(x_ref[...], w_ref[...])
    acc = acc * r_ref[...]
    if relu2:
        acc = jnp.maximum(acc, 0.0)
        acc = acc * acc
    o_ref[...] = acc.astype(o_ref.dtype)


def normed_matmul(hg, ssq, w, layer, out_dtype, tm, tn, *, w_is_nk, relu2, name):
    m, k = hg.shape
    n = w.shape[1] if w_is_nk else w.shape[2]
    tm, tn = _tile(m, tm), _tile(n, tn)
    if w_is_nk:
        w_spec = pl.BlockSpec((None, tn, k), lambda i, j: (layer, j, 0))
    else:
        w_spec = pl.BlockSpec((None, k, tn), lambda i, j: (layer, 0, j))
    return pl.pallas_call(
        functools.partial(_normed_kernel, w_is_nk=w_is_nk, relu2=relu2),
        grid=(m // tm, n // tn),
        in_specs=[pl.BlockSpec((tm, k), lambda i, j: (i, 0)),
                  w_spec,
                  pl.BlockSpec((tm, ssq.shape[1]), lambda i, j: (i, 0))],
        out_specs=pl.BlockSpec((tm, tn), lambda i, j: (i, j)),
        out_shape=jax.ShapeDtypeStruct((m, n), out_dtype),
        scratch_shapes=[pltpu.VMEM((tm, 1), F32)],
        compiler_params=_cparams(("parallel", "arbitrary")),
        name=name,
    )(hg, w, ssq)


def _emit_stats(h_new, g_ref, hg_ref, sq_ref):
    hg_ref[...] = (h_new * g_ref[...]).astype(hg_ref.dtype)
    sq_ref[...] = _fold_lanes(h_new * h_new)


def _residual_kernel(x_ref, w_ref, h_ref, g_ref, o_ref, hg_ref, sq_ref):
    h_new = h_ref[...] + _dot(x_ref[...], w_ref[...])
    o_ref[...] = h_new
    _emit_stats(h_new, g_ref, hg_ref, sq_ref)


def _stats_out(m, n, tm, tn, index):
    specs = [pl.BlockSpec((tm, tn), index), pl.BlockSpec((tm, tn), index),
             pl.BlockSpec((tm, LANES), index)]
    shapes = [jax.ShapeDtypeStruct((m, n), F32), jax.ShapeDtypeStruct((m, n), BF16),
              jax.ShapeDtypeStruct((m, (n // tn) * LANES), F32)]
    return specs, shapes


def residual_matmul(x, w, layer, h, gain, tm, tn, name):
    m, k = x.shape
    n = w.shape[2]
    tm, tn = _tile(m, tm), _tile(n, tn)
    out_specs, out_shape = _stats_out(m, n, tm, tn, lambda i, j: (i, j))
    return pl.pallas_call(
        _residual_kernel,
        grid=(m // tm, n // tn),
        in_specs=[pl.BlockSpec((tm, k), lambda i, j: (i, 0)),
                  pl.BlockSpec((None, k, tn), lambda i, j: (layer, 0, j)),
                  pl.BlockSpec((tm, tn), lambda i, j: (i, j)),
                  pl.BlockSpec((1, tn), lambda i, j: (0, j))],
        out_specs=out_specs,
        out_shape=out_shape,
        compiler_params=_cparams(("parallel", "parallel")),
        name=name,
    )(x, w, h, gain.reshape(1, n))


def _down_kernel(u_ref, w_ref, h_ref, g_ref, o_ref, hg_ref, sq_ref, acc_ref):
    kk = pl.program_id(2)

    @pl.when(kk == 0)
    def _():
        acc_ref[...] = h_ref[...]

    acc_ref[...] += _dot(u_ref[...], w_ref[...])

    @pl.when(kk == pl.num_programs(2) - 1)
    def _():
        h_new = acc_ref[...]
        o_ref[...] = h_new
        _emit_stats(h_new, g_ref, hg_ref, sq_ref)


def down_proj(u, w, layer, h, gain, tm, tn, tk):
    m, k = u.shape
    n = w.shape[2]
    tm, tn, tk = _tile(m, tm), _tile(n, tn), _tile(k, tk)
    out_specs, out_shape = _stats_out(m, n, tm, tn, lambda i, j, kk: (i, j))
    return pl.pallas_call(
        _down_kernel,
        grid=(m // tm, n // tn, k // tk),
        in_specs=[pl.BlockSpec((tm, tk), lambda i, j, kk: (i, kk)),
                  pl.BlockSpec((None, tk, tn), lambda i, j, kk: (layer, kk, j)),
                  pl.BlockSpec((tm, tn), lambda i, j, kk: (i, j)),
                  pl.BlockSpec((1, tn), lambda i, j, kk: (0, j))],
        out_specs=out_specs,
        out_shape=out_shape,
        scratch_shapes=[pltpu.VMEM((tm, tn), F32)],
        compiler_params=_cparams(("parallel", "parallel", "arbitrary")),
        name="down_proj",
    )(u, w, h, gain.reshape(1, n))


def _merge_kernel(ya_ref, yb_ref, wa_ref, wb_ref, ga_ref, gb_ref, o_ref):
    a = _dot(ya_ref[...], wa_ref[...])
    b = _dot(yb_ref[...], wb_ref[...])
    o_ref[...] = (_sigmoid(ga_ref[...]) * a + _sigmoid(gb_ref[...]) * b).astype(o_ref.dtype)


def merge(ya, yb, wa, wb, layer, proj, ga_col, gb_col, tm, tn):
    m, k = ya.shape
    n = wa.shape[2]
    tm, tn = _tile(m, tm), _tile(n, tn)
    ga_blk, gb_blk = ga_col // tn, gb_col // tn
    assert ga_blk * tn == ga_col and gb_blk * tn == gb_col
    return pl.pallas_call(
        _merge_kernel,
        grid=(m // tm, n // tn),
        in_specs=[pl.BlockSpec((tm, k), lambda i, j: (i, 0)),
                  pl.BlockSpec((tm, k), lambda i, j: (i, 0)),
                  pl.BlockSpec((None, k, tn), lambda i, j: (layer, 0, j)),
                  pl.BlockSpec((None, k, tn), lambda i, j: (layer, 0, j)),
                  pl.BlockSpec((tm, tn), lambda i, j: (i, ga_blk + j)),
                  pl.BlockSpec((tm, tn), lambda i, j: (i, gb_blk + j))],
        out_specs=pl.BlockSpec((tm, tn), lambda i, j: (i, j)),
        out_shape=jax.ShapeDtypeStruct((m, n), BF16),
        compiler_params=_cparams(("parallel", "parallel")),
        name="merge",
    )(ya, yb, wa, wb, proj, proj)


def _mlstm_kernel(qk_ref, v_ref, og_ref, gc_ref, gr_ref, convw_ref, bc_ref, br_ref,
                  o_ref, xbuf, c_st, n_st, m_st):
    L = MLSTM_CHUNK

    @pl.when(pl.program_id(1) == 0)
    def _():
        xbuf[0:CONV_HALO, :] = jnp.zeros((CONV_HALO, 2 * QK), F32)
        c_st[...] = jnp.zeros_like(c_st)
        n_st[...] = jnp.zeros_like(n_st)
        m_st[...] = jnp.zeros_like(m_st)

    xbuf[CONV_HALO:CONV_HALO + L, :] = qk_ref[0]
    w = convw_ref[...]
    acc = w[CONV_W - 1:CONV_W, :] * xbuf[CONV_HALO:CONV_HALO + L, :]
    for j in range(CONV_W - 1):
        back = CONV_W - 1 - j
        acc = acc + w[j:j + 1, :] * xbuf[CONV_HALO - back:CONV_HALO - back + L, :]
    xbuf[0:CONV_HALO, :] = xbuf[L:L + CONV_HALO, :]
    qk = acc * _sigmoid(acc)

    g_col = gc_ref[0] + bc_ref[...]
    g_row = gr_ref[0] + br_ref[...]
    b_col = _tri_left(_tri(L, True), _log_sigmoid(g_col))
    b_row = _tri_right(_log_sigmoid(g_row), _tri(L, False))

    row = lax.broadcasted_iota(jnp.int32, (L, L), 0)
    col = lax.broadcasted_iota(jnp.int32, (L, L), 1)
    causal = row >= col

    for h in range(HEADS):
        q_f = qk[:, h * DK:(h + 1) * DK]
        q = q_f.astype(BF16)
        k_f = qk[:, QK + h * DK:QK + (h + 1) * DK] * (DK ** -0.5)
        k = k_f.astype(BF16)
        v_f = v_ref[0, :, h * DV:(h + 1) * DV]
        v = v_f.astype(BF16)
        li_c = g_col[:, h:h + 1]
        li_r = g_row[h:h + 1, :]
        b_c = b_col[:, HEADS + h:HEADS + h + 1]
        b_r = b_row[HEADS + h:HEADS + h + 1, :]
        m_prev = m_st[h:h + 1, 0:1]
        ct = c_st[h]
        n_row = n_st[h:h + 1, :]

        log_d = jnp.where(causal, b_c - b_r + li_r, -jnp.inf)
        m_t = jnp.maximum(b_c + m_prev, jnp.max(log_d, axis=1, keepdims=True))
        s = _dot_nt(q, k) * jnp.exp(log_d - m_t)
        carry_w = jnp.exp(b_c + m_prev - m_t)
        num = _dot(s.astype(BF16), v) + carry_w * _dot(q, ct.astype(BF16))
        qn = jnp.sum(q_f * n_row, axis=1, keepdims=True)
        den = jnp.sum(s, axis=1, keepdims=True) + carry_w * qn
        hh = num / jnp.maximum(jnp.abs(den), jnp.exp(-m_t))
        og = og_ref[0, :, h * DV:(h + 1) * DV]
        o_ref[0, :, h * DV:(h + 1) * DV] = (hh * _sigmoid(og)).astype(o_ref.dtype)

        g_tot = b_c[L - 1:L, :]
        log_w = g_tot - b_c + li_c
        m_new = jnp.maximum(g_tot + m_prev, jnp.max(log_w, axis=0, keepdims=True))
        wgt = jnp.exp(log_w - m_new)
        decay = jnp.exp(g_tot + m_prev - m_new)
        kv = _dot_tn(k, (wgt * v_f).astype(BF16))
        c_st[h] = decay * ct + kv
        n_st[h:h + 1, :] = decay * n_row + jnp.sum(wgt * k_f, axis=0, keepdims=True)
        m_st[h:h + 1, :] = jnp.broadcast_to(m_new, (1, m_st.shape[1]))


def mlstm_mixer(proj, small, small_t, conv_w, bias_col, bias_row, qk_col, v_col, og_col):
    bsz, t, _ = proj.shape
    L = MLSTM_CHUNK
    assert t % L == 0
    qk_blk, v_blk, og_blk = qk_col // (2 * QK), v_col // VW, og_col // VW
    return pl.pallas_call(
        _mlstm_kernel,
        grid=(bsz, t // L),
        in_specs=[pl.BlockSpec((1, L, 2 * QK), lambda b, c: (b, c, qk_blk)),
                  pl.BlockSpec((1, L, VW), lambda b, c: (b, c, v_blk)),
                  pl.BlockSpec((1, L, VW), lambda b, c: (b, c, og_blk)),
                  pl.BlockSpec((1, L, SMALL_W), lambda b, c: (b, c, 0)),
                  pl.BlockSpec((1, 8, L), lambda b, c: (b, 0, c)),
                  pl.BlockSpec((CONV_W, 2 * QK), lambda b, c: (0, 0)),
                  pl.BlockSpec((1, SMALL_W), lambda b, c: (0, 0)),
                  pl.BlockSpec((8, L), lambda b, c: (0, 0))],
        out_specs=pl.BlockSpec((1, L, VW), lambda b, c: (b, c, 0)),
        out_shape=jax.ShapeDtypeStruct((bsz, t, VW), BF16),
        scratch_shapes=[pltpu.VMEM((L + CONV_HALO, 2 * QK), F32),
                        pltpu.VMEM((HEADS, DK, DV), F32),
                        pltpu.VMEM((8, DK), F32),
                        pltpu.VMEM((8, 128), F32)],
        compiler_params=_cparams(("parallel", "arbitrary")),
        name="mlstm",
    )(proj, proj, proj, small, small_t, conv_w, bias_col, bias_row)


def _half_block_ref(x, w):
    n_rows, n = x.shape
    if 2 * w >= 8:
        x3 = x.reshape(n_rows // (2 * w), 2 * w, n)
        return jnp.broadcast_to(x3[:, w - 1:w, :], x3.shape).reshape(n_rows, n)
    x3 = x.reshape(n_rows // 8, 8, n)
    sub = lax.broadcasted_iota(jnp.int32, x3.shape, 1)
    groups = 8 // (2 * w)
    pick = lambda g: jnp.broadcast_to(x3[:, g * 2 * w + w - 1:g * 2 * w + w, :], x3.shape)
    out = pick(groups - 1)
    for g in reversed(range(groups - 1)):
        out = jnp.where(sub < (g + 1) * 2 * w, pick(g), out)
    return out.reshape(n_rows, n)


def _gla_kernel(qk_ref, v_ref, og_ref, gc_ref, wg_ref, bg_ref, gn_ref, o_ref, s_st):
    L = GLA_CHUNK

    @pl.when(pl.program_id(1) == 0)
    def _():
        s_st[...] = jnp.zeros_like(s_st)

    z = _dot(gc_ref[0].astype(BF16), wg_ref[...]) + bg_ref[...]
    log_a = _log_sigmoid(z) * (1.0 / GATE_TAU)
    cum = _tri_left(_tri(L, True), log_a)

    row = lax.broadcasted_iota(jnp.int32, (L, L), 0)
    col = lax.broadcasted_iota(jnp.int32, (L, L), 1)
    level = jnp.where(row > col, 31 - lax.clz(row ^ col), -1)
    diag = row == col
    eye = (lax.broadcasted_iota(jnp.int32, (DK, DK), 0)
           == lax.broadcasted_iota(jnp.int32, (DK, DK), 1))
    n_levels = L.bit_length() - 1

    for h in range(HEADS):
        q_f = qk_ref[0, :, h * DK:(h + 1) * DK] * (DK ** -0.5)
        k_f = qk_ref[0, :, QK + h * DK:QK + (h + 1) * DK]
        v = v_ref[0, :, h * DV:(h + 1) * DV].astype(BF16)
        cum_h = cum[:, h * DK:(h + 1) * DK]

        att = jnp.where(diag, jnp.sum(q_f * k_f, axis=1, keepdims=True), 0.0)
        for lw in range(n_levels):
            f = jnp.exp(-jnp.abs(cum_h - _half_block_ref(cum_h, 1 << lw)))
            p = _dot_nt((q_f * f).astype(BF16), (k_f * f).astype(BF16))
            att = att + jnp.where(level == lw, p, 0.0)

        st = s_st[h]
        o = _dot(att.astype(BF16), v) + _dot((q_f * jnp.exp(cum_h)).astype(BF16), st.astype(BF16))
        last = cum_h[L - 1:L, :]
        k_dec = (k_f * jnp.exp(last - cum_h)).astype(BF16)
        kv = _dot_tn(k_dec, v)
        decay_col = jnp.sum(jnp.where(eye, jnp.exp(last), 0.0), axis=1, keepdims=True)
        s_st[h] = decay_col * st + kv

        o = o * lax.rsqrt(jnp.mean(o * o, axis=1, keepdims=True) + EPS)
        og = og_ref[0, :, h * DV:(h + 1) * DV]
        y = o * gn_ref[:, h * DV:(h + 1) * DV] * (og * _sigmoid(og))
        o_ref[0, :, h * DV:(h + 1) * DV] = y.astype(o_ref.dtype)


def gla_mixer(proj, small, wg_pad, bg, gn, qk_col, v_col, og_col):
    bsz, t, _ = proj.shape
    L = GLA_CHUNK
    assert t % L == 0
    qk_blk, v_blk, og_blk = qk_col // (2 * QK), v_col // VW, og_col // VW
    return pl.pallas_call(
        _gla_kernel,
        grid=(bsz, t // L),
        in_specs=[pl.BlockSpec((1, L, 2 * QK), lambda b, c: (b, c, qk_blk)),
                  pl.BlockSpec((1, L, VW), lambda b, c: (b, c, v_blk)),
                  pl.BlockSpec((1, L, VW), lambda b, c: (b, c, og_blk)),
                  pl.BlockSpec((1, L, SMALL_W), lambda b, c: (b, c, 0)),
                  pl.BlockSpec((SMALL_W, QK), lambda b, c: (0, 0)),
                  pl.BlockSpec((1, QK), lambda b, c: (0, 0)),
                  pl.BlockSpec((1, VW), lambda b, c: (0, 0))],
        out_specs=pl.BlockSpec((1, L, VW), lambda b, c: (b, c, 0)),
        out_shape=jax.ShapeDtypeStruct((bsz, t, VW), BF16),
        scratch_shapes=[pltpu.VMEM((HEADS, DK, DV), F32)],
        compiler_params=_cparams(("parallel", "arbitrary")),
        name="gla",
    )(proj, proj, proj, small, wg_pad, bg, gn)


def _projection_layout(d_model):
    a_w = 2 * QK + 2 * VW
    src_a = 0
    src_if = a_w
    src_b = src_if + 2 * HEADS
    src_z = src_b + a_w
    src_g = src_z + GATE_RANK
    end = src_g + 2 * d_model
    main = [(src_a, src_if), (src_b, src_z), (src_g, end)]
    small = [(src_if, src_b), (src_z, src_g)]
    cols = dict(a_qk=0, a_v=2 * QK, a_og=2 * QK + VW, b_qk=a_w, b_v=a_w + 2 * QK,
                b_og=a_w + 2 * QK + VW, ga=2 * a_w, gb=2 * a_w + d_model)
    return main, small, cols, end


def _layer(h, hg, ssq, bsz, t_pad, l, p, w, cols, tiles, next_gain):
    m, d = h.shape
    proj = normed_matmul(hg, ssq, w['in_main_t'], l, F32, tiles['tm'], tiles['tn_in'],
                         w_is_nk=True, relu2=False, name="in_proj")
    small_p = normed_matmul(hg, ssq, w['in_small_t'], l, F32, tiles['tm'], SMALL_W,
                            w_is_nk=True, relu2=False, name="in_proj_small")

    proj3 = proj.reshape(bsz, t_pad, proj.shape[1])
    small3 = small_p.reshape(bsz, t_pad, SMALL_W)
    small_t = jnp.swapaxes(small3[:, :, :2 * HEADS], 1, 2)
    bias_col = jnp.pad(p['b_if'], (0, SMALL_W - 2 * HEADS)).reshape(1, SMALL_W)
    bias_row = jnp.broadcast_to(p['b_if'].reshape(2 * HEADS, 1), (2 * HEADS, MLSTM_CHUNK))
    ya = mlstm_mixer(proj3, small3, small_t, p['conv_qk'], bias_col, bias_row,
                     cols['a_qk'], cols['a_v'], cols['a_og'])

    wg_pad = jnp.zeros((SMALL_W, QK), F32).at[2 * HEADS:2 * HEADS + GATE_RANK].set(p['w_gla_gate'])
    yb = gla_mixer(proj3, small3, wg_pad.astype(BF16), p['b_gla_gate'].reshape(1, QK),
                   p['norm_gla'].reshape(1, VW), cols['b_qk'], cols['b_v'], cols['b_og'])

    merged = merge(ya.reshape(m, VW), yb.reshape(m, VW), w['br_a'], w['br_b'], l, proj,
                   cols['ga'], cols['gb'], tiles['tm'], tiles['tn_merge'])
    h, hg, ssq = residual_matmul(merged, w['out'], l, h, p['norm_mlp'], tiles['tm'], tiles['tn'],
                                 name="out_proj")
    u = normed_matmul(hg, ssq, w['up'], l, BF16, tiles['tm'], tiles['tn'],
                      w_is_nk=False, relu2=True, name="up_proj")
    return down_proj(u, w['down'], l, h, next_gain, tiles['tm'], tiles['tn_down'], tiles['tk'])


def kernel(x, meta, norm_mix, w_in, conv_qk, b_if, w_gla_gate, b_gla_gate, norm_gla, w_br_a,
           w_br_b, w_out, norm_mlp, w_up, w_down, norm_final):
    bsz, seq, d = x.shape
    depth = w_in.shape[0]
    n = N_META + seq
    t_pad = -(-n // T_ALIGN) * T_ALIGN
    h = jnp.concatenate([jnp.broadcast_to(meta.astype(x.dtype)[None], (bsz, N_META, d)), x,
                         jnp.zeros((bsz, t_pad - n, d), x.dtype)], axis=1)
    h = h.reshape(bsz * t_pad, d)

    main, small, cols, p_in = _projection_layout(d)
    assert w_in.shape[2] == p_in
    w_in_t = jnp.swapaxes(w_in, 1, 2)
    in_small_t = jnp.concatenate([w_in_t[:, a:b] for a, b in small], axis=1)
    in_small_t = jnp.pad(in_small_t, ((0, 0), (0, SMALL_W - in_small_t.shape[1]), (0, 0)))
    w = dict(in_main_t=jnp.concatenate([w_in_t[:, a:b] for a, b in main], axis=1).astype(BF16),
             in_small_t=in_small_t.astype(BF16), br_a=w_br_a.astype(BF16),
             br_b=w_br_b.astype(BF16), out=w_out.astype(BF16), up=w_up.astype(BF16),
             down=w_down.astype(BF16))

    tiles = dict(tm=1280, tn=512, tn_in=512, tn_merge=1024, tn_down=1024, tk=2048)
    hg, ssq = gain_stats(h, norm_mix[0])
    for l in range(depth):
        p = dict(conv_qk=conv_qk[l], b_if=b_if[l], w_gla_gate=w_gla_gate[l],
                 b_gla_gate=b_gla_gate[l], norm_gla=norm_gla[l], norm_mlp=norm_mlp[l])
        next_gain = norm_mix[l + 1] if l + 1 < depth else norm_final
        h, hg, ssq = _layer(h, hg, ssq, bsz, t_pad, l, p, w, cols, tiles, next_gain)
    out = rmsnorm(h, norm_final, x.dtype)
    return out.reshape(bsz, t_pad, d)[:, N_META:N_META + seq]
```

```python
import functools

import jax
import jax.numpy as jnp
from jax import lax
from jax.experimental import pallas as pl
from jax.experimental.pallas import tpu as pltpu

F32 = jnp.float32
BF16 = jnp.bfloat16

N_META = 16
EPS = 1e-6
HEADS = 4
DK = 128
DV = 256
QK = HEADS * DK
VW = HEADS * DV
CONV_W = 4
GATE_RANK = 16
GATE_TAU = 16.0
LANES = 128
SMALL_W = LANES
CONV_HALO = 8

MLSTM_CHUNK = 128
GLA_CHUNK = 64
T_ALIGN = 128

VMEM_LIMIT_BYTES = 60 * 1024 * 1024


def _cparams(sem):
    return pltpu.CompilerParams(dimension_semantics=sem, vmem_limit_bytes=VMEM_LIMIT_BYTES)


def _tile(n, pref):
    if n <= pref:
        return n
    t = (pref // LANES) * LANES
    while t >= LANES:
        if n % t == 0:
            return t
        t -= LANES
    raise ValueError(f"no 128-aligned tile for {n}")


def _sigmoid(x):
    return 1.0 / (1.0 + jnp.exp(-x))


def _log_sigmoid(x):
    return jnp.minimum(x, 0.0) - jnp.log1p(jnp.exp(-jnp.abs(x)))


def _split3(x):
    hi = x.astype(BF16)
    r = x - hi.astype(F32)
    mid = r.astype(BF16)
    lo = (r - mid.astype(F32)).astype(BF16)
    return hi, mid, lo


def _dot(a, b):
    return jnp.dot(a, b, preferred_element_type=F32)


def _dot_nt(a, b):
    return lax.dot_general(a, b, (((1,), (1,)), ((), ())), preferred_element_type=F32)


def _dot_tn(a, b):
    return lax.dot_general(a, b, (((0,), (0,)), ((), ())), preferred_element_type=F32)


def _tri_left(tri, x):
    hi, mid, lo = _split3(x)
    return (_dot(tri, lo) + _dot(tri, mid)) + _dot(tri, hi)


def _tri_right(x, tri):
    hi, mid, lo = _split3(x)
    return (_dot(lo, tri) + _dot(mid, tri)) + _dot(hi, tri)


def _tri(n, lower):
    r = lax.broadcasted_iota(jnp.int32, (n, n), 0)
    c = lax.broadcasted_iota(jnp.int32, (n, n), 1)
    keep = (r >= c) if lower else (r <= c)
    return jnp.where(keep, 1.0, 0.0).astype(BF16)


def _fold_lanes(x):
    out = x[:, 0:LANES]
    for c in range(1, x.shape[1] // LANES):
        out = out + x[:, c * LANES:(c + 1) * LANES]
    return out


def _row_scale(ssq, d):
    return lax.rsqrt(jnp.sum(ssq, axis=1, keepdims=True) * (1.0 / d) + EPS)


def _rmsnorm_kernel(x_ref, g_ref, o_ref):
    x = x_ref[...]
    ms = jnp.mean(x * x, axis=-1, keepdims=True)
    o_ref[...] = (x * lax.rsqrt(ms + EPS) * g_ref[...]).astype(o_ref.dtype)


def rmsnorm(x, g, out_dtype, rows=256):
    m, d = x.shape
    tr = _tile(m, rows)
    return pl.pallas_call(
        _rmsnorm_kernel,
        grid=(m // tr,),
        in_specs=[pl.BlockSpec((tr, d), lambda i: (i, 0)),
                  pl.BlockSpec((1, d), lambda i: (0, 0))],
        out_specs=pl.BlockSpec((tr, d), lambda i: (i, 0)),
        out_shape=jax.ShapeDtypeStruct((m, d), out_dtype),
        compiler_params=_cparams(("parallel",)),
        name="rmsnorm",
    )(x, g.reshape(1, d))


def _gain_stats_kernel(x_ref, g_ref, hg_ref, sq_ref):
    x = x_ref[...]
    hg_ref[...] = (x * g_ref[...]).astype(hg_ref.dtype)
    sq_ref[...] = _fold_lanes(x * x)


def gain_stats(x, g, rows=256):
    m, d = x.shape
    tr = _tile(m, rows)
    return pl.pallas_call(
        _gain_stats_kernel,
        grid=(m // tr,),
        in_specs=[pl.BlockSpec((tr, d), lambda i: (i, 0)),
                  pl.BlockSpec((1, d), lambda i: (0, 0))],
        out_specs=[pl.BlockSpec((tr, d), lambda i: (i, 0)),
                   pl.BlockSpec((tr, LANES), lambda i: (i, 0))],
        out_shape=[jax.ShapeDtypeStruct((m, d), BF16),
                   jax.ShapeDtypeStruct((m, LANES), F32)],
        compiler_params=_cparams(("parallel",)),
        name="gain_stats",
    )(x, g.reshape(1, d))


PACK_HALO = 32


def _pack_rows_kernel(a_ref, b_ref, o_ref, *, shifts):
    rows = o_ref.shape[0]
    j = pl.program_id(1)
    for idx, (first, shift) in enumerate(shifts):
        cond = j >= first
        if idx + 1 < len(shifts):
            cond = jnp.logical_and(cond, j < shifts[idx + 1][0])

        @pl.when(cond)
        def _(shift=shift):
            if shift == 0:
                o_ref[...] = a_ref[...].astype(o_ref.dtype)
            else:
                wide = jnp.concatenate([a_ref[shift:rows, :], b_ref[0:shift, :]], axis=0)
                o_ref[...] = wide.astype(o_ref.dtype)


def pack_rows(w, segments, tile):
    layers, r, k = w.shape
    shifts, start = [], 0
    for a, b in segments:
        assert start % tile == 0 and (a - start) % 8 == 0 and 0 <= a - start <= PACK_HALO
        shifts.append((start // tile, a - start))
        start += b - a
    assert start % tile == 0 and tile % PACK_HALO == 0
    return pl.pallas_call(
        functools.partial(_pack_rows_kernel, shifts=tuple(shifts)),
        grid=(layers, start // tile),
        in_specs=[pl.BlockSpec((None, tile, k), lambda l, j: (l, j, 0)),
                  pl.BlockSpec((None, PACK_HALO, k), lambda l, j: (l, (j + 1) * (tile // PACK_HALO), 0))],
        out_specs=pl.BlockSpec((None, tile, k), lambda l, j: (l, j, 0)),
        out_shape=jax.ShapeDtypeStruct((layers, start, k), BF16),
        compiler_params=_cparams(("parallel", "parallel")),
        name="pack_rows",
    )(w, w)


def _normed_kernel(x_ref, w_ref, ssq_ref, o_ref, r_ref, *, w_is_nk, relu2):
    @pl.when(pl.program_id(1) == 0)
    def _():
        r_ref[...] = _row_scale(ssq_ref[...], x_ref.shape[1])

    acc = _dot_nt(x_ref[...], w_ref[...]) if w_is_nk else _dot(x_ref[...], w_ref[...])
    acc = acc * r_ref[...]
    if relu2:
        acc = jnp.maximum(acc, 0.0)
        acc = acc * acc
    o_ref[...] = acc.astype(o_ref.dtype)


def normed_matmul(hg, ssq, w, layer, out_dtype, tm, tn, *, w_is_nk, relu2, name):
    m, k = hg.shape
    n = w.shape[1] if w_is_nk else w.shape[2]
    tm, tn = _tile(m, tm), _tile(n, tn)
    if w_is_nk:
        w_spec = pl.BlockSpec((None, tn, k), lambda i, j: (layer, j, 0))
    else:
        w_spec = pl.BlockSpec((None, k, tn), lambda i, j: (layer, 0, j))
    return pl.pallas_call(
        functools.partial(_normed_kernel, w_is_nk=w_is_nk, relu2=relu2),
        grid=(m // tm, n // tn),
        in_specs=[pl.BlockSpec((tm, k), lambda i, j: (i, 0)),
                  w_spec,
                  pl.BlockSpec((tm, ssq.shape[1]), lambda i, j: (i, 0))],
        out_specs=pl.BlockSpec((tm, tn), lambda i, j: (i, j)),
        out_shape=jax.ShapeDtypeStruct((m, n), out_dtype),
        scratch_shapes=[pltpu.VMEM((tm, 1), F32)],
        compiler_params=_cparams(("parallel", "arbitrary")),
        name=name,
    )(hg, w, ssq)


def _emit_stats(h_new, g_ref, hg_ref, sq_ref, first):
    hg_ref[...] = (h_new * g_ref[...]).astype(hg_ref.dtype)
    part = _fold_lanes(h_new * h_new)

    @pl.when(first)
    def _():
        sq_ref[...] = part

    @pl.when(jnp.logical_not(first))
    def _():
        sq_ref[...] += part


def _residual_kernel(x_ref, w_ref, h_ref, g_ref, o_ref, hg_ref, sq_ref):
    h_new = h_ref[...] + _dot(x_ref[...], w_ref[...])
    o_ref[...] = h_new
    _emit_stats(h_new, g_ref, hg_ref, sq_ref, pl.program_id(1) == 0)


def _stats_out(m, n, tm, tn, tile_index, row_index):
    specs = [pl.BlockSpec((tm, tn), tile_index), pl.BlockSpec((tm, tn), tile_index),
             pl.BlockSpec((tm, LANES), row_index)]
    shapes = [jax.ShapeDtypeStruct((m, n), F32), jax.ShapeDtypeStruct((m, n), BF16),
              jax.ShapeDtypeStruct((m, LANES), F32)]
    return specs, shapes


def residual_matmul(x, w, layer, h, gain, tm, tn, name):
    m, k = x.shape
    n = w.shape[2]
    tm, tn = _tile(m, tm), _tile(n, tn)
    out_specs, out_shape = _stats_out(m, n, tm, tn, lambda i, j: (i, j), lambda i, j: (i, 0))
    return pl.pallas_call(
        _residual_kernel,
        grid=(m // tm, n // tn),
        in_specs=[pl.BlockSpec((tm, k), lambda i, j: (i, 0)),
                  pl.BlockSpec((None, k, tn), lambda i, j: (layer, 0, j)),
                  pl.BlockSpec((tm, tn), lambda i, j: (i, j)),
                  pl.BlockSpec((1, tn), lambda i, j: (0, j))],
        out_specs=out_specs,
        out_shape=out_shape,
        compiler_params=_cparams(("parallel", "arbitrary")),
        name=name,
    )(x, w, h, gain.reshape(1, n))


def _down_kernel(u_ref, w_ref, h_ref, g_ref, o_ref, hg_ref, sq_ref, acc_ref):
    kk = pl.program_id(2)
    first_tile = pl.program_id(1) == 0

    @pl.when(kk == 0)
    def _():
        acc_ref[...] = h_ref[...]

    acc_ref[...] += _dot(u_ref[...], w_ref[...])

    @pl.when(kk == pl.num_programs(2) - 1)
    def _():
        h_new = acc_ref[...]
        o_ref[...] = h_new
        _emit_stats(h_new, g_ref, hg_ref, sq_ref, first_tile)


def down_proj(u, w, layer, h, gain, tm, tn, tk):
    m, k = u.shape
    n = w.shape[2]
    tm, tn, tk = _tile(m, tm), _tile(n, tn), _tile(k, tk)
    out_specs, out_shape = _stats_out(m, n, tm, tn, lambda i, j, kk: (i, j),
                                      lambda i, j, kk: (i, 0))
    return pl.pallas_call(
        _down_kernel,
        grid=(m // tm, n // tn, k // tk),
        in_specs=[pl.BlockSpec((tm, tk), lambda i, j, kk: (i, kk)),
                  pl.BlockSpec((None, tk, tn), lambda i, j, kk: (layer, kk, j)),
                  pl.BlockSpec((tm, tn), lambda i, j, kk: (i, j)),
                  pl.BlockSpec((1, tn), lambda i, j, kk: (0, j))],
        out_specs=out_specs,
        out_shape=out_shape,
        scratch_shapes=[pltpu.VMEM((tm, tn), F32)],
        compiler_params=_cparams(("parallel", "arbitrary", "arbitrary")),
        name="down_proj",
    )(u, w, h, gain.reshape(1, n))


def _merge_kernel(ya_ref, yb_ref, wa_ref, wb_ref, ga_ref, gb_ref, o_ref):
    a = _dot(ya_ref[...], wa_ref[...])
    b = _dot(yb_ref[...], wb_ref[...])
    o_ref[...] = (_sigmoid(ga_ref[...]) * a + _sigmoid(gb_ref[...]) * b).astype(o_ref.dtype)


def merge(ya, yb, wa, wb, layer, proj, ga_col, gb_col, tm, tn):
    m, k = ya.shape
    n = wa.shape[2]
    tm, tn = _tile(m, tm), _tile(n, tn)
    ga_blk, gb_blk = ga_col // tn, gb_col // tn
    assert ga_blk * tn == ga_col and gb_blk * tn == gb_col
    return pl.pallas_call(
        _merge_kernel,
        grid=(m // tm, n // tn),
        in_specs=[pl.BlockSpec((tm, k), lambda i, j: (i, 0)),
                  pl.BlockSpec((tm, k), lambda i, j: (i, 0)),
                  pl.BlockSpec((None, k, tn), lambda i, j: (layer, 0, j)),
                  pl.BlockSpec((None, k, tn), lambda i, j: (layer, 0, j)),
                  pl.BlockSpec((tm, tn), lambda i, j: (i, ga_blk + j)),
                  pl.BlockSpec((tm, tn), lambda i, j: (i, gb_blk + j))],
        out_specs=pl.BlockSpec((tm, tn), lambda i, j: (i, j)),
        out_shape=jax.ShapeDtypeStruct((m, n), BF16),
        compiler_params=_cparams(("parallel", "parallel")),
        name="merge",
    )(ya, yb, wa, wb, proj, proj)


def _mlstm_sequence(qk_ref, v_ref, og_ref, gc_ref, gr_ref, convw_ref, bc_ref, br_ref,
                    o_ref, xbuf, c_st, n_st, m_st):
    L = MLSTM_CHUNK

    @pl.when(pl.program_id(0) == 0)
    def _():
        xbuf[0:CONV_HALO, :] = jnp.zeros((CONV_HALO, 2 * QK), F32)
        c_st[...] = jnp.zeros_like(c_st)
        n_st[...] = jnp.zeros_like(n_st)
        m_st[...] = jnp.zeros_like(m_st)

    xbuf[CONV_HALO:CONV_HALO + L, :] = qk_ref[...]
    w = convw_ref[...]
    acc = w[CONV_W - 1:CONV_W, :] * xbuf[CONV_HALO:CONV_HALO + L, :]
    for j in range(CONV_W - 1):
        back = CONV_W - 1 - j
        acc = acc + w[j:j + 1, :] * xbuf[CONV_HALO - back:CONV_HALO - back + L, :]
    xbuf[0:CONV_HALO, :] = xbuf[L:L + CONV_HALO, :]
    qk = acc * _sigmoid(acc)

    g_col = gc_ref[...] + bc_ref[...]
    g_row = gr_ref[...] + br_ref[...]
    b_col = _tri_left(_tri(L, True), _log_sigmoid(g_col))
    b_row = _tri_right(_log_sigmoid(g_row), _tri(L, False))

    row = lax.broadcasted_iota(jnp.int32, (L, L), 0)
    col = lax.broadcasted_iota(jnp.int32, (L, L), 1)
    causal = row >= col

    for h in range(HEADS):
        q_f = qk[:, h * DK:(h + 1) * DK]
        q = q_f.astype(BF16)
        k_f = qk[:, QK + h * DK:QK + (h + 1) * DK] * (DK ** -0.5)
        k = k_f.astype(BF16)
        v_f = v_ref[:, h * DV:(h + 1) * DV]
        v = v_f.astype(BF16)
        li_c = g_col[:, h:h + 1]
        li_r = g_row[h:h + 1, :]
        b_c = b_col[:, HEADS + h:HEADS + h + 1]
        b_r = b_row[HEADS + h:HEADS + h + 1, :]
        m_prev = m_st[h:h + 1, 0:1]
        ct = c_st[h]
        n_row = n_st[h:h + 1, :]

        log_d = jnp.where(causal, b_c - b_r + li_r, -jnp.inf)
        m_t = jnp.maximum(b_c + m_prev, jnp.max(log_d, axis=1, keepdims=True))
        s = _dot_nt(q, k) * jnp.exp(log_d - m_t)
        carry_w = jnp.exp(b_c + m_prev - m_t)
        num = _dot(s.astype(BF16), v) + carry_w * _dot(q, ct.astype(BF16))
        qn = jnp.sum(q_f * n_row, axis=1, keepdims=True)
        den = jnp.sum(s, axis=1, keepdims=True) + carry_w * qn
        hh = num / jnp.maximum(jnp.abs(den), jnp.exp(-m_t))
        og = og_ref[:, h * DV:(h + 1) * DV]
        o_ref[:, h * DV:(h + 1) * DV] = (hh * _sigmoid(og)).astype(o_ref.dtype)

        g_tot = b_c[L - 1:L, :]
        log_w = g_tot - b_c + li_c
        m_new = jnp.maximum(g_tot + m_prev, jnp.max(log_w, axis=0, keepdims=True))
        wgt = jnp.exp(log_w - m_new)
        decay = jnp.exp(g_tot + m_prev - m_new)
        kv = _dot_tn(k, (wgt * v_f).astype(BF16))
        c_st[h] = decay * ct + kv
        n_st[h:h + 1, :] = decay * n_row + jnp.sum(wgt * k_f, axis=0, keepdims=True)
        m_st[h:h + 1, :] = jnp.broadcast_to(m_new, (1, m_st.shape[1]))


def _mlstm_kernel(qk_ref, v_ref, og_ref, gc_ref, gr_ref, convw_ref, bc_ref, br_ref,
                  o_ref, xbuf, c_st, n_st, m_st):
    for b in range(qk_ref.shape[0]):
        _mlstm_sequence(qk_ref.at[b], v_ref.at[b], og_ref.at[b], gc_ref.at[b], gr_ref.at[b],
                        convw_ref, bc_ref, br_ref, o_ref.at[b], xbuf.at[b], c_st.at[b],
                        n_st.at[b], m_st.at[b])


def mlstm_mixer(proj, small, small_t, conv_w, bias_col, bias_row, qk_col, v_col, og_col):
    bsz, t, _ = proj.shape
    L = MLSTM_CHUNK
    assert t % L == 0
    qk_blk, v_blk, og_blk = qk_col // (2 * QK), v_col // VW, og_col // VW
    return pl.pallas_call(
        _mlstm_kernel,
        grid=(t // L,),
        in_specs=[pl.BlockSpec((bsz, L, 2 * QK), lambda c: (0, c, qk_blk)),
                  pl.BlockSpec((bsz, L, VW), lambda c: (0, c, v_blk)),
                  pl.BlockSpec((bsz, L, VW), lambda c: (0, c, og_blk)),
                  pl.BlockSpec((bsz, L, SMALL_W), lambda c: (0, c, 0)),
                  pl.BlockSpec((bsz, 8, L), lambda c: (0, 0, c)),
                  pl.BlockSpec((CONV_W, 2 * QK), lambda c: (0, 0)),
                  pl.BlockSpec((1, SMALL_W), lambda c: (0, 0)),
                  pl.BlockSpec((8, L), lambda c: (0, 0))],
        out_specs=pl.BlockSpec((bsz, L, VW), lambda c: (0, c, 0)),
        out_shape=jax.ShapeDtypeStruct((bsz, t, VW), BF16),
        scratch_shapes=[pltpu.VMEM((bsz, L + CONV_HALO, 2 * QK), F32),
                        pltpu.VMEM((bsz, HEADS, DK, DV), F32),
                        pltpu.VMEM((bsz, 8, DK), F32),
                        pltpu.VMEM((bsz, 8, 128), F32)],
        compiler_params=_cparams(("arbitrary",)),
        name="mlstm",
    )(proj, proj, proj, small, small_t, conv_w, bias_col, bias_row)


def _half_block_ref(x, w):
    n_rows, n = x.shape
    if 2 * w >= 8:
        x3 = x.reshape(n_rows // (2 * w), 2 * w, n)
        return jnp.broadcast_to(x3[:, w - 1:w, :], x3.shape).reshape(n_rows, n)
    x3 = x.reshape(n_rows // 8, 8, n)
    sub = lax.broadcasted_iota(jnp.int32, x3.shape, 1)
    groups = 8 // (2 * w)
    pick = lambda g: jnp.broadcast_to(x3[:, g * 2 * w + w - 1:g * 2 * w + w, :], x3.shape)
    out = pick(groups - 1)
    for g in reversed(range(groups - 1)):
        out = jnp.where(sub < (g + 1) * 2 * w, pick(g), out)
    return out.reshape(n_rows, n)


def _gla_sequence(qk_ref, v_ref, og_ref, gc_ref, wg_ref, bg_ref, gn_ref, o_ref, s_st):
    L = GLA_CHUNK

    @pl.when(pl.program_id(0) == 0)
    def _():
        s_st[...] = jnp.zeros_like(s_st)

    z = _dot(gc_ref[...].astype(BF16), wg_ref[...]) + bg_ref[...]
    log_a = _log_sigmoid(z) * (1.0 / GATE_TAU)
    cum = _tri_left(_tri(L, True), log_a)

    row = lax.broadcasted_iota(jnp.int32, (L, L), 0)
    col = lax.broadcasted_iota(jnp.int32, (L, L), 1)
    level = jnp.where(row > col, 31 - lax.clz(row ^ col), -1)
    diag = row == col
    eye = (lax.broadcasted_iota(jnp.int32, (DK, DK), 0)
           == lax.broadcasted_iota(jnp.int32, (DK, DK), 1))
    n_levels = L.bit_length() - 1

    for h in range(HEADS):
        q_f = qk_ref[:, h * DK:(h + 1) * DK] * (DK ** -0.5)
        k_f = qk_ref[:, QK + h * DK:QK + (h + 1) * DK]
        v = v_ref[:, h * DV:(h + 1) * DV].astype(BF16)
        cum_h = cum[:, h * DK:(h + 1) * DK]

        att = jnp.where(diag, jnp.sum(q_f * k_f, axis=1, keepdims=True), 0.0)
        for lw in range(n_levels):
            f = jnp.exp(-jnp.abs(cum_h - _half_block_ref(cum_h, 1 << lw)))
            p = _dot_nt((q_f * f).astype(BF16), (k_f * f).astype(BF16))
            att = att + jnp.where(level == lw, p, 0.0)

        st = s_st[h]
        o = _dot(att.astype(BF16), v) + _dot((q_f * jnp.exp(cum_h)).astype(BF16), st.astype(BF16))
        last = cum_h[L - 1:L, :]
        k_dec = (k_f * jnp.exp(last - cum_h)).astype(BF16)
        kv = _dot_tn(k_dec, v)
        decay_col = jnp.sum(jnp.where(eye, jnp.exp(last), 0.0), axis=1, keepdims=True)
        s_st[h] = decay_col * st + kv

        o = o * lax.rsqrt(jnp.mean(o * o, axis=1, keepdims=True) + EPS)
        og = og_ref[:, h * DV:(h + 1) * DV]
        y = o * gn_ref[:, h * DV:(h + 1) * DV] * (og * _sigmoid(og))
        o_ref[:, h * DV:(h + 1) * DV] = y.astype(o_ref.dtype)


def _gla_kernel(qk_ref, v_ref, og_ref, gc_ref, wg_ref, bg_ref, gn_ref, o_ref, s_st):
    for b in range(qk_ref.shape[0]):
        _gla_sequence(qk_ref.at[b], v_ref.at[b], og_ref.at[b], gc_ref.at[b], wg_ref, bg_ref,
                      gn_ref, o_ref.at[b], s_st.at[b])


def gla_mixer(proj, small, wg_pad, bg, gn, qk_col, v_col, og_col):
    bsz, t, _ = proj.shape
    L = GLA_CHUNK
    assert t % L == 0
    qk_blk, v_blk, og_blk = qk_col // (2 * QK), v_col // VW, og_col // VW
    return pl.pallas_call(
        _gla_kernel,
        grid=(t // L,),
        in_specs=[pl.BlockSpec((bsz, L, 2 * QK), lambda c: (0, c, qk_blk)),
                  pl.BlockSpec((bsz, L, VW), lambda c: (0, c, v_blk)),
                  pl.BlockSpec((bsz, L, VW), lambda c: (0, c, og_blk)),
                  pl.BlockSpec((bsz, L, SMALL_W), lambda c: (0, c, 0)),
                  pl.BlockSpec((SMALL_W, QK), lambda c: (0, 0)),
                  pl.BlockSpec((1, QK), lambda c: (0, 0)),
                  pl.BlockSpec((1, VW), lambda c: (0, 0))],
        out_specs=pl.BlockSpec((bsz, L, VW), lambda c: (0, c, 0)),
        out_shape=jax.ShapeDtypeStruct((bsz, t, VW), BF16),
        scratch_shapes=[pltpu.VMEM((bsz, HEADS, DK, DV), F32)],
        compiler_params=_cparams(("arbitrary",)),
        name="gla",
    )(proj, proj, proj, small, wg_pad, bg, gn)


def _projection_layout(d_model):
    a_w = 2 * QK + 2 * VW
    src_a = 0
    src_if = a_w
    src_b = src_if + 2 * HEADS
    src_z = src_b + a_w
    src_g = src_z + GATE_RANK
    end = src_g + 2 * d_model
    main = [(src_a, src_if), (src_b, src_z), (src_g, end)]
    small = [(src_if, src_b), (src_z, src_g)]
    cols = dict(a_qk=0, a_v=2 * QK, a_og=2 * QK + VW, b_qk=a_w, b_v=a_w + 2 * QK,
                b_og=a_w + 2 * QK + VW, ga=2 * a_w, gb=2 * a_w + d_model)
    return main, small, cols, end


def _layer(h, hg, ssq, bsz, t_pad, l, p, w, cols, tiles, next_gain):
    m, d = h.shape
    proj = normed_matmul(hg, ssq, w['in_main_t'], l, F32, tiles['tm'], tiles['tn_in'],
                         w_is_nk=True, relu2=False, name="in_proj")
    small_p = normed_matmul(hg, ssq, w['in_small_t'], l, F32, tiles['tm'], SMALL_W,
                            w_is_nk=True, relu2=False, name="in_proj_small")

    proj3 = proj.reshape(bsz, t_pad, proj.shape[1])
    small3 = small_p.reshape(bsz, t_pad, SMALL_W)
    small_t = jnp.swapaxes(small3[:, :, :2 * HEADS], 1, 2)
    bias_col = jnp.pad(p['b_if'], (0, SMALL_W - 2 * HEADS)).reshape(1, SMALL_W)
    bias_row = jnp.broadcast_to(p['b_if'].reshape(2 * HEADS, 1), (2 * HEADS, MLSTM_CHUNK))
    ya = mlstm_mixer(proj3, small3, small_t, p['conv_qk'], bias_col, bias_row,
                     cols['a_qk'], cols['a_v'], cols['a_og'])

    wg_pad = jnp.zeros((SMALL_W, QK), F32).at[2 * HEADS:2 * HEADS + GATE_RANK].set(p['w_gla_gate'])
    yb = gla_mixer(proj3, small3, wg_pad.astype(BF16), p['b_gla_gate'].reshape(1, QK),
                   p['norm_gla'].reshape(1, VW), cols['b_qk'], cols['b_v'], cols['b_og'])

    merged = merge(ya.reshape(m, VW), yb.reshape(m, VW), w['br_a'], w['br_b'], l, proj,
                   cols['ga'], cols['gb'], tiles['tm'], tiles['tn_merge'])
    h, hg, ssq = residual_matmul(merged, w['out'], l, h, p['norm_mlp'], tiles['tm'],
                                 tiles['tn_out'], name="out_proj")
    u = normed_matmul(hg, ssq, w['up'], l, BF16, tiles['tm'], tiles['tn_up'],
                      w_is_nk=False, relu2=True, name="up_proj")
    return down_proj(u, w['down'], l, h, next_gain, tiles['tm'], tiles['tn_down'], tiles['tk'])


def kernel(x, meta, norm_mix, w_in, conv_qk, b_if, w_gla_gate, b_gla_gate, norm_gla, w_br_a,
           w_br_b, w_out, norm_mlp, w_up, w_down, norm_final):
    bsz, seq, d = x.shape
    depth = w_in.shape[0]
    n = N_META + seq
    t_pad = -(-n // T_ALIGN) * T_ALIGN
    h = jnp.concatenate([jnp.broadcast_to(meta.astype(x.dtype)[None], (bsz, N_META, d)), x,
                         jnp.zeros((bsz, t_pad - n, d), x.dtype)], axis=1)
    h = h.reshape(bsz * t_pad, d)

    main, small, cols, p_in = _projection_layout(d)
    assert w_in.shape[2] == p_in
    w_in_t = jnp.swapaxes(w_in, 1, 2)
    in_small_t = jnp.concatenate([w_in_t[:, a:b] for a, b in small], axis=1)
    in_small_t = jnp.pad(in_small_t, ((0, 0), (0, SMALL_W - in_small_t.shape[1]), (0, 0)))
    w = dict(in_main_t=pack_rows(w_in_t, main, tile=512),
             in_small_t=in_small_t.astype(BF16), br_a=w_br_a.astype(BF16),
             br_b=w_br_b.astype(BF16), out=w_out.astype(BF16), up=w_up.astype(BF16),
             down=w_down.astype(BF16))

    tiles = dict(tm=1280, tn_in=1024, tn_merge=1024, tn_out=512, tn_up=1024, tn_down=1024, tk=2048)
    hg, ssq = gain_stats(h, norm_mix[0])
    for l in range(depth):
        p = dict(conv_qk=conv_qk[l], b_if=b_if[l], w_gla_gate=w_gla_gate[l],
                 b_gla_gate=b_gla_gate[l], norm_gla=norm_gla[l], norm_mlp=norm_mlp[l])
        next_gain = norm_mix[l + 1] if l + 1 < depth else norm_final
        h, hg, ssq = _layer(h, hg, ssq, bsz, t_pad, l, p, w, cols, tiles, next_gain)
    out = rmsnorm(h, norm_final, x.dtype)
    return out.reshape(bsz, t_pad, d)[:, N_META:N_META + seq]
```

```python
import functools

import jax
import jax.numpy as jnp
from jax import lax
from jax.experimental import pallas as pl
from jax.experimental.pallas import tpu as pltpu

F32 = jnp.float32
BF16 = jnp.bfloat16

N_META = 16
EPS = 1e-6
HEADS = 4
DK = 128
DV = 256
QK = HEADS * DK
VW = HEADS * DV
CONV_W = 4
GATE_RANK = 16
GATE_TAU = 16.0
LANES = 128
SMALL_W = LANES
CONV_HALO = 8

MLSTM_CHUNK = 128
GLA_CHUNK = 64
T_ALIGN = 128

VMEM_LIMIT_BYTES = 60 * 1024 * 1024


def _cparams(sem):
    return pltpu.CompilerParams(dimension_semantics=sem, vmem_limit_bytes=VMEM_LIMIT_BYTES)


def _tile(n, pref):
    if n <= pref:
        return n
    t = (pref // LANES) * LANES
    while t >= LANES:
        if n % t == 0:
            return t
        t -= LANES
    raise ValueError(f"no 128-aligned tile for {n}")


def _sigmoid(x):
    return 1.0 / (1.0 + jnp.exp(-x))


def _log_sigmoid(x):
    return jnp.minimum(x, 0.0) - jnp.log1p(jnp.exp(-jnp.abs(x)))


def _split3(x):
    hi = x.astype(BF16)
    r = x - hi.astype(F32)
    mid = r.astype(BF16)
    lo = (r - mid.astype(F32)).astype(BF16)
    return hi, mid, lo


def _dot(a, b):
    return jnp.dot(a, b, preferred_element_type=F32)


def _dot_nt(a, b):
    return lax.dot_general(a, b, (((1,), (1,)), ((), ())), preferred_element_type=F32)


def _dot_tn(a, b):
    return lax.dot_general(a, b, (((0,), (0,)), ((), ())), preferred_element_type=F32)


def _tri_left(tri, x):
    hi, mid, lo = _split3(x)
    return (_dot(tri, lo) + _dot(tri, mid)) + _dot(tri, hi)


def _tri_right(x, tri):
    hi, mid, lo = _split3(x)
    return (_dot(lo, tri) + _dot(mid, tri)) + _dot(hi, tri)


def _tri(n, lower):
    r = lax.broadcasted_iota(jnp.int32, (n, n), 0)
    c = lax.broadcasted_iota(jnp.int32, (n, n), 1)
    keep = (r >= c) if lower else (r <= c)
    return jnp.where(keep, 1.0, 0.0).astype(BF16)


def _fold_lanes(x):
    out = x[:, 0:LANES]
    for c in range(1, x.shape[1] // LANES):
        out = out + x[:, c * LANES:(c + 1) * LANES]
    return out


def _row_scale(ssq, d):
    return lax.rsqrt(jnp.sum(ssq, axis=1, keepdims=True) * (1.0 / d) + EPS)


def _rmsnorm_kernel(x_ref, g_ref, o_ref):
    x = x_ref[...]
    ms = jnp.mean(x * x, axis=-1, keepdims=True)
    o_ref[...] = (x * lax.rsqrt(ms + EPS) * g_ref[...]).astype(o_ref.dtype)


def _gain_stats_kernel(x_ref, g_ref, hg_ref, sq_ref):
    x = x_ref[...]
    hg_ref[...] = (x * g_ref[...]).astype(hg_ref.dtype)
    sq_ref[...] = _fold_lanes(x * x)


def gain_stats(x, g, rows=256):
    m, d = x.shape
    tr = _tile(m, rows)
    return pl.pallas_call(
        _gain_stats_kernel,
        grid=(m // tr,),
        in_specs=[pl.BlockSpec((tr, d), lambda i: (i, 0)),
                  pl.BlockSpec((1, d), lambda i: (0, 0))],
        out_specs=[pl.BlockSpec((tr, d), lambda i: (i, 0)),
                   pl.BlockSpec((tr, LANES), lambda i: (i, 0))],
        out_shape=[jax.ShapeDtypeStruct((m, d), BF16),
                   jax.ShapeDtypeStruct((m, LANES), F32)],
        compiler_params=_cparams(("parallel",)),
        name="gain_stats",
    )(x, g.reshape(1, d))


def final_norm(h, g, bsz, first_row, n_rows, rows=256):
    m, d = h.shape
    t = m // bsz
    tr = _tile(n_rows, rows)
    assert first_row % 8 == 0 and t % 8 == 0
    window = pl.BlockSpec((pl.Element(tr), pl.Element(d)),
                          lambda b, i: (pl.multiple_of(b * t + first_row + i * tr, 8), 0))
    return pl.pallas_call(
        _rmsnorm_kernel,
        grid=(bsz, n_rows // tr),
        in_specs=[window, pl.BlockSpec((1, d), lambda b, i: (0, 0))],
        out_specs=pl.BlockSpec((None, tr, d), lambda b, i: (b, i, 0)),
        out_shape=jax.ShapeDtypeStruct((bsz, n_rows, d), h.dtype),
        compiler_params=_cparams(("parallel", "parallel")),
        name="final_norm",
    )(h, g.reshape(1, d))


PACK_HALO = 32


def _pack_rows_kernel(a_ref, b_ref, o_ref, *, shifts):
    rows = o_ref.shape[0]
    j = pl.program_id(1)
    for idx, (first, shift) in enumerate(shifts):
        cond = j >= first
        if idx + 1 < len(shifts):
            cond = jnp.logical_and(cond, j < shifts[idx + 1][0])

        @pl.when(cond)
        def _(shift=shift):
            if shift == 0:
                o_ref[...] = a_ref[...].astype(o_ref.dtype)
            else:
                wide = jnp.concatenate([a_ref[shift:rows, :], b_ref[0:shift, :]], axis=0)
                o_ref[...] = wide.astype(o_ref.dtype)


def pack_rows(w, segments, tile):
    layers, r, k = w.shape
    shifts, start = [], 0
    for a, b in segments:
        assert start % tile == 0 and (a - start) % 8 == 0 and 0 <= a - start <= PACK_HALO
        shifts.append((start // tile, a - start))
        start += b - a
    assert start % tile == 0 and tile % PACK_HALO == 0
    return pl.pallas_call(
        functools.partial(_pack_rows_kernel, shifts=tuple(shifts)),
        grid=(layers, start // tile),
        in_specs=[pl.BlockSpec((None, tile, k), lambda l, j: (l, j, 0)),
                  pl.BlockSpec((None, PACK_HALO, k), lambda l, j: (l, (j + 1) * (tile // PACK_HALO), 0))],
        out_specs=pl.BlockSpec((None, tile, k), lambda l, j: (l, j, 0)),
        out_shape=jax.ShapeDtypeStruct((layers, start, k), BF16),
        compiler_params=_cparams(("parallel", "parallel")),
        name="pack_rows",
    )(w, w)


def _gather_rows_kernel(*refs):
    *in_refs, o_ref = refs
    pad = o_ref.shape[0] - 8 * len(in_refs)
    parts = [r[...] for r in in_refs] + [jnp.zeros((pad, o_ref.shape[1]), F32)]
    o_ref[...] = jnp.concatenate(parts, axis=0).astype(o_ref.dtype)


def gather_rows(w, segments, n_out):
    layers, _, k = w.shape
    starts = []
    for a, b in segments:
        assert a % 8 == 0 and (b - a) % 8 == 0
        starts += list(range(a // 8, b // 8))
    return pl.pallas_call(
        _gather_rows_kernel,
        grid=(layers,),
        in_specs=[pl.BlockSpec((None, 8, k), functools.partial(lambda l, s: (l, s, 0), s=s))
                  for s in starts],
        out_specs=pl.BlockSpec((None, n_out, k), lambda l: (l, 0, 0)),
        out_shape=jax.ShapeDtypeStruct((layers, n_out, k), BF16),
        compiler_params=_cparams(("parallel",)),
        name="gather_rows",
    )(*([w] * len(starts)))


def _normed_kernel(x_ref, w_ref, ssq_ref, o_ref, r_ref, *, w_is_nk, relu2):
    @pl.when(pl.program_id(1) == 0)
    def _():
        r_ref[...] = _row_scale(ssq_ref[...], x_ref.shape[1])

    acc = _dot_nt(x_ref[...], w_ref[...]) if w_is_nk else _dot(x_ref[...], w_ref[...])
    acc = acc * r_ref[...]
    if relu2:
        acc = jnp.maximum(acc, 0.0)
        acc = acc * acc
    o_ref[...] = acc.astype(o_ref.dtype)


def normed_matmul(hg, ssq, w, layer, out_dtype, tm, tn, *, w_is_nk, relu2, name):
    m, k = hg.shape
    n = w.shape[1] if w_is_nk else w.shape[2]
    tm, tn = _tile(m, tm), _tile(n, tn)
    if w_is_nk:
        w_spec = pl.BlockSpec((None, tn, k), lambda i, j: (layer, j, 0))
    else:
        w_spec = pl.BlockSpec((None, k, tn), lambda i, j: (layer, 0, j))
    return pl.pallas_call(
        functools.partial(_normed_kernel, w_is_nk=w_is_nk, relu2=relu2),
        grid=(m // tm, n // tn),
        in_specs=[pl.BlockSpec((tm, k), lambda i, j: (i, 0)),
                  w_spec,
                  pl.BlockSpec((tm, ssq.shape[1]), lambda i, j: (i, 0))],
        out_specs=pl.BlockSpec((tm, tn), lambda i, j: (i, j)),
        out_shape=jax.ShapeDtypeStruct((m, n), out_dtype),
        scratch_shapes=[pltpu.VMEM((tm, 1), F32)],
        compiler_params=_cparams(("parallel", "arbitrary")),
        name=name,
    )(hg, w, ssq)


def _emit_stats(h_new, g_ref, hg_ref, sq_ref, first):
    hg_ref[...] = (h_new * g_ref[...]).astype(hg_ref.dtype)
    part = _fold_lanes(h_new * h_new)

    @pl.when(first)
    def _():
        sq_ref[...] = part

    @pl.when(jnp.logical_not(first))
    def _():
        sq_ref[...] += part


def _residual_kernel(x_ref, w_ref, h_ref, g_ref, o_ref, hg_ref, sq_ref):
    h_new = h_ref[...] + _dot(x_ref[...], w_ref[...])
    o_ref[...] = h_new
    _emit_stats(h_new, g_ref, hg_ref, sq_ref, pl.program_id(1) == 0)


def _stats_out(m, n, tm, tn, tile_index, row_index):
    specs = [pl.BlockSpec((tm, tn), tile_index), pl.BlockSpec((tm, tn), tile_index),
             pl.BlockSpec((tm, LANES), row_index)]
    shapes = [jax.ShapeDtypeStruct((m, n), F32), jax.ShapeDtypeStruct((m, n), BF16),
              jax.ShapeDtypeStruct((m, LANES), F32)]
    return specs, shapes


def residual_matmul(x, w, layer, h, gain, tm, tn, name):
    m, k = x.shape
    n = w.shape[2]
    tm, tn = _tile(m, tm), _tile(n, tn)
    out_specs, out_shape = _stats_out(m, n, tm, tn, lambda i, j: (i, j), lambda i, j: (i, 0))
    return pl.pallas_call(
        _residual_kernel,
        grid=(m // tm, n // tn),
        in_specs=[pl.BlockSpec((tm, k), lambda i, j: (i, 0)),
                  pl.BlockSpec((None, k, tn), lambda i, j: (layer, 0, j)),
                  pl.BlockSpec((tm, tn), lambda i, j: (i, j)),
                  pl.BlockSpec((1, tn), lambda i, j: (0, j))],
        out_specs=out_specs,
        out_shape=out_shape,
        compiler_params=_cparams(("parallel", "arbitrary")),
        name=name,
    )(x, w, h, gain.reshape(1, n))


def _down_kernel(u_ref, w_ref, h_ref, g_ref, o_ref, hg_ref, sq_ref, acc_ref):
    kk = pl.program_id(2)
    first_tile = pl.program_id(1) == 0

    @pl.when(kk == 0)
    def _():
        acc_ref[...] = h_ref[...]

    acc_ref[...] += _dot(u_ref[...], w_ref[...])

    @pl.when(kk == pl.num_programs(2) - 1)
    def _():
        h_new = acc_ref[...]
        o_ref[...] = h_new
        _emit_stats(h_new, g_ref, hg_ref, sq_ref, first_tile)


def down_proj(u, w, layer, h, gain, tm, tn, tk):
    m, k = u.shape
    n = w.shape[2]
    tm, tn, tk = _tile(m, tm), _tile(n, tn), _tile(k, tk)
    out_specs, out_shape = _stats_out(m, n, tm, tn, lambda i, j, kk: (i, j),
                                      lambda i, j, kk: (i, 0))
    return pl.pallas_call(
        _down_kernel,
        grid=(m // tm, n // tn, k // tk),
        in_specs=[pl.BlockSpec((tm, tk), lambda i, j, kk: (i, kk)),
                  pl.BlockSpec((None, tk, tn), lambda i, j, kk: (layer, kk, j)),
                  pl.BlockSpec((tm, tn), lambda i, j, kk: (i, j)),
                  pl.BlockSpec((1, tn), lambda i, j, kk: (0, j))],
        out_specs=out_specs,
        out_shape=out_shape,
        scratch_shapes=[pltpu.VMEM((tm, tn), F32)],
        compiler_params=_cparams(("parallel", "arbitrary", "arbitrary")),
        name="down_proj",
    )(u, w, h, gain.reshape(1, n))


def _merge_kernel(ya_ref, yb_ref, wa_ref, wb_ref, ga_ref, gb_ref, o_ref):
    a = _dot(ya_ref[...], wa_ref[...])
    b = _dot(yb_ref[...], wb_ref[...])
    o_ref[...] = (_sigmoid(ga_ref[...]) * a + _sigmoid(gb_ref[...]) * b).astype(o_ref.dtype)


def merge(ya, yb, wa, wb, layer, proj, ga_col, gb_col, tm, tn):
    m, k = ya.shape
    n = wa.shape[2]
    tm, tn = _tile(m, tm), _tile(n, tn)
    ga_blk, gb_blk = ga_col // tn, gb_col // tn
    assert ga_blk * tn == ga_col and gb_blk * tn == gb_col
    return pl.pallas_call(
        _merge_kernel,
        grid=(m // tm, n // tn),
        in_specs=[pl.BlockSpec((tm, k), lambda i, j: (i, 0)),
                  pl.BlockSpec((tm, k), lambda i, j: (i, 0)),
                  pl.BlockSpec((None, k, tn), lambda i, j: (layer, 0, j)),
                  pl.BlockSpec((None, k, tn), lambda i, j: (layer, 0, j)),
                  pl.BlockSpec((tm, tn), lambda i, j: (i, ga_blk + j)),
                  pl.BlockSpec((tm, tn), lambda i, j: (i, gb_blk + j))],
        out_specs=pl.BlockSpec((tm, tn), lambda i, j: (i, j)),
        out_shape=jax.ShapeDtypeStruct((m, n), BF16),
        compiler_params=_cparams(("parallel", "parallel")),
        name="merge",
    )(ya, yb, wa, wb, proj, proj)


def _mlstm_sequence(qk_ref, v_ref, og_ref, gc_ref, gr_ref, convw_ref, bc_ref, br_ref,
                    o_ref, xbuf, c_st, n_st, m_st):
    L = MLSTM_CHUNK

    @pl.when(pl.program_id(0) == 0)
    def _():
        xbuf[0:CONV_HALO, :] = jnp.zeros((CONV_HALO, 2 * QK), F32)
        c_st[...] = jnp.zeros_like(c_st)
        n_st[...] = jnp.zeros_like(n_st)
        m_st[...] = jnp.zeros_like(m_st)

    xbuf[CONV_HALO:CONV_HALO + L, :] = qk_ref[...]
    w = convw_ref[...]
    acc = w[CONV_W - 1:CONV_W, :] * xbuf[CONV_HALO:CONV_HALO + L, :]
    for j in range(CONV_W - 1):
        back = CONV_W - 1 - j
        acc = acc + w[j:j + 1, :] * xbuf[CONV_HALO - back:CONV_HALO - back + L, :]
    xbuf[0:CONV_HALO, :] = xbuf[L:L + CONV_HALO, :]
    qk = acc * _sigmoid(acc)

    g_col = gc_ref[...] + bc_ref[...]
    g_row = gr_ref[...] + br_ref[...]
    b_col = _tri_left(_tri(L, True), _log_sigmoid(g_col))
    b_row = _tri_right(_log_sigmoid(g_row), _tri(L, False))

    row = lax.broadcasted_iota(jnp.int32, (L, L), 0)
    col = lax.broadcasted_iota(jnp.int32, (L, L), 1)
    causal = row >= col

    for h in range(HEADS):
        q_f = qk[:, h * DK:(h + 1) * DK]
        q = q_f.astype(BF16)
        k_f = qk[:, QK + h * DK:QK + (h + 1) * DK] * (DK ** -0.5)
        k = k_f.astype(BF16)
        v_f = v_ref[:, h * DV:(h + 1) * DV]
        v = v_f.astype(BF16)
        li_c = g_col[:, h:h + 1]
        li_r = g_row[h:h + 1, :]
        b_c = b_col[:, HEADS + h:HEADS + h + 1]
        b_r = b_row[HEADS + h:HEADS + h + 1, :]
        m_prev = m_st[h:h + 1, 0:1]
        ct = c_st[h]
        n_row = n_st[h:h + 1, :]

        log_d = jnp.where(causal, b_c - b_r + li_r, -jnp.inf)
        m_t = jnp.maximum(b_c + m_prev, jnp.max(log_d, axis=1, keepdims=True))
        s = _dot_nt(q, k) * jnp.exp(log_d - m_t)
        carry_w = jnp.exp(b_c + m_prev - m_t)
        num = _dot(s.astype(BF16), v) + carry_w * _dot(q, ct.astype(BF16))
        qn = jnp.sum(q_f * n_row, axis=1, keepdims=True)
        den = jnp.sum(s, axis=1, keepdims=True) + carry_w * qn
        hh = num / jnp.maximum(jnp.abs(den), jnp.exp(-m_t))
        og = og_ref[:, h * DV:(h + 1) * DV]
        o_ref[:, h * DV:(h + 1) * DV] = (hh * _sigmoid(og)).astype(o_ref.dtype)

        g_tot = b_c[L - 1:L, :]
        log_w = g_tot - b_c + li_c
        m_new = jnp.maximum(g_tot + m_prev, jnp.max(log_w, axis=0, keepdims=True))
        wgt = jnp.exp(log_w - m_new)
        decay = jnp.exp(g_tot + m_prev - m_new)
        kv = _dot_tn(k, (wgt * v_f).astype(BF16))
        c_st[h] = decay * ct + kv
        n_st[h:h + 1, :] = decay * n_row + jnp.sum(wgt * k_f, axis=0, keepdims=True)
        m_st[h:h + 1, :] = jnp.broadcast_to(m_new, (1, m_st.shape[1]))


def _mlstm_kernel(qk_ref, v_ref, og_ref, gc_ref, gr_ref, convw_ref, bc_ref, br_ref,
                  o_ref, xbuf, c_st, n_st, m_st):
    for b in range(qk_ref.shape[0]):
        _mlstm_sequence(qk_ref.at[b], v_ref.at[b], og_ref.at[b], gc_ref.at[b], gr_ref.at[b],
                        convw_ref, bc_ref, br_ref, o_ref.at[b], xbuf.at[b], c_st.at[b],
                        n_st.at[b], m_st.at[b])


def mlstm_mixer(proj, small, small_t, conv_w, bias_col, bias_row, qk_col, v_col, og_col):
    bsz, t, _ = proj.shape
    L = MLSTM_CHUNK
    assert t % L == 0
    qk_blk, v_blk, og_blk = qk_col // (2 * QK), v_col // VW, og_col // VW
    return pl.pallas_call(
        _mlstm_kernel,
        grid=(t // L,),
        in_specs=[pl.BlockSpec((bsz, L, 2 * QK), lambda c: (0, c, qk_blk)),
                  pl.BlockSpec((bsz, L, VW), lambda c: (0, c, v_blk)),
                  pl.BlockSpec((bsz, L, VW), lambda c: (0, c, og_blk)),
                  pl.BlockSpec((bsz, L, SMALL_W), lambda c: (0, c, 0)),
                  pl.BlockSpec((bsz, 8, L), lambda c: (0, 0, c)),
                  pl.BlockSpec((CONV_W, 2 * QK), lambda c: (0, 0)),
                  pl.BlockSpec((1, SMALL_W), lambda c: (0, 0)),
                  pl.BlockSpec((8, L), lambda c: (0, 0))],
        out_specs=pl.BlockSpec((bsz, L, VW), lambda c: (0, c, 0)),
        out_shape=jax.ShapeDtypeStruct((bsz, t, VW), BF16),
        scratch_shapes=[pltpu.VMEM((bsz, L + CONV_HALO, 2 * QK), F32),
                        pltpu.VMEM((bsz, HEADS, DK, DV), F32),
                        pltpu.VMEM((bsz, 8, DK), F32),
                        pltpu.VMEM((bsz, 8, 128), F32)],
        compiler_params=_cparams(("arbitrary",)),
        name="mlstm",
    )(proj, proj, proj, small, small_t, conv_w, bias_col, bias_row)


def _half_block_ref(x, w):
    n_rows, n = x.shape
    if 2 * w >= 8:
        x3 = x.reshape(n_rows // (2 * w), 2 * w, n)
        return jnp.broadcast_to(x3[:, w - 1:w, :], x3.shape).reshape(n_rows, n)
    x3 = x.reshape(n_rows // 8, 8, n)
    sub = lax.broadcasted_iota(jnp.int32, x3.shape, 1)
    groups = 8 // (2 * w)
    pick = lambda g: jnp.broadcast_to(x3[:, g * 2 * w + w - 1:g * 2 * w + w, :], x3.shape)
    out = pick(groups - 1)
    for g in reversed(range(groups - 1)):
        out = jnp.where(sub < (g + 1) * 2 * w, pick(g), out)
    return out.reshape(n_rows, n)


def _gla_sequence(qk_ref, v_ref, og_ref, gc_ref, wg_ref, bg_ref, gn_ref, o_ref, s_st):
    L = GLA_CHUNK

    @pl.when(pl.program_id(0) == 0)
    def _():
        s_st[...] = jnp.zeros_like(s_st)

    z = _dot(gc_ref[...].astype(BF16), wg_ref[...]) + bg_ref[...]
    log_a = _log_sigmoid(z) * (1.0 / GATE_TAU)
    cum = _tri_left(_tri(L, True), log_a)

    row = lax.broadcasted_iota(jnp.int32, (L, L), 0)
    col = lax.broadcasted_iota(jnp.int32, (L, L), 1)
    level = jnp.where(row > col, 31 - lax.clz(row ^ col), -1)
    diag = row == col
    eye = (lax.broadcasted_iota(jnp.int32, (DK, DK), 0)
           == lax.broadcasted_iota(jnp.int32, (DK, DK), 1))
    n_levels = L.bit_length() - 1

    for h in range(HEADS):
        q_f = qk_ref[:, h * DK:(h + 1) * DK] * (DK ** -0.5)
        k_f = qk_ref[:, QK + h * DK:QK + (h + 1) * DK]
        v = v_ref[:, h * DV:(h + 1) * DV].astype(BF16)
        cum_h = cum[:, h * DK:(h + 1) * DK]

        att = jnp.where(diag, jnp.sum(q_f * k_f, axis=1, keepdims=True), 0.0)
        for lw in range(n_levels):
            f = jnp.exp(-jnp.abs(cum_h - _half_block_ref(cum_h, 1 << lw)))
            p = _dot_nt((q_f * f).astype(BF16), (k_f * f).astype(BF16))
            att = att + jnp.where(level == lw, p, 0.0)

        st = s_st[h]
        o = _dot(att.astype(BF16), v) + _dot((q_f * jnp.exp(cum_h)).astype(BF16), st.astype(BF16))
        last = cum_h[L - 1:L, :]
        k_dec = (k_f * jnp.exp(last - cum_h)).astype(BF16)
        kv = _dot_tn(k_dec, v)
        decay_col = jnp.sum(jnp.where(eye, jnp.exp(last), 0.0), axis=1, keepdims=True)
        s_st[h] = decay_col * st + kv

        o = o * lax.rsqrt(jnp.mean(o * o, axis=1, keepdims=True) + EPS)
        og = og_ref[:, h * DV:(h + 1) * DV]
        y = o * gn_ref[:, h * DV:(h + 1) * DV] * (og * _sigmoid(og))
        o_ref[:, h * DV:(h + 1) * DV] = y.astype(o_ref.dtype)


def _gla_kernel(qk_ref, v_ref, og_ref, gc_ref, wg_ref, bg_ref, gn_ref, o_ref, s_st):
    for b in range(qk_ref.shape[0]):
        _gla_sequence(qk_ref.at[b], v_ref.at[b], og_ref.at[b], gc_ref.at[b], wg_ref, bg_ref,
                      gn_ref, o_ref.at[b], s_st.at[b])


def gla_mixer(proj, small, wg_pad, bg, gn, qk_col, v_col, og_col):
    bsz, t, _ = proj.shape
    L = GLA_CHUNK
    assert t % L == 0
    qk_blk, v_blk, og_blk = qk_col // (2 * QK), v_col // VW, og_col // VW
    return pl.pallas_call(
        _gla_kernel,
        grid=(t // L,),
        in_specs=[pl.BlockSpec((bsz, L, 2 * QK), lambda c: (0, c, qk_blk)),
                  pl.BlockSpec((bsz, L, VW), lambda c: (0, c, v_blk)),
                  pl.BlockSpec((bsz, L, VW), lambda c: (0, c, og_blk)),
                  pl.BlockSpec((bsz, L, SMALL_W), lambda c: (0, c, 0)),
                  pl.BlockSpec((SMALL_W, QK), lambda c: (0, 0)),
                  pl.BlockSpec((1, QK), lambda c: (0, 0)),
                  pl.BlockSpec((1, VW), lambda c: (0, 0))],
        out_specs=pl.BlockSpec((bsz, L, VW), lambda c: (0, c, 0)),
        out_shape=jax.ShapeDtypeStruct((bsz, t, VW), BF16),
        scratch_shapes=[pltpu.VMEM((bsz, HEADS, DK, DV), F32)],
        compiler_params=_cparams(("arbitrary",)),
        name="gla",
    )(proj, proj, proj, small, wg_pad, bg, gn)


def _projection_layout(d_model):
    a_w = 2 * QK + 2 * VW
    src_a = 0
    src_if = a_w
    src_b = src_if + 2 * HEADS
    src_z = src_b + a_w
    src_g = src_z + GATE_RANK
    end = src_g + 2 * d_model
    main = [(src_a, src_if), (src_b, src_z), (src_g, end)]
    small = [(src_if, src_b), (src_z, src_g)]
    cols = dict(a_qk=0, a_v=2 * QK, a_og=2 * QK + VW, b_qk=a_w, b_v=a_w + 2 * QK,
                b_og=a_w + 2 * QK + VW, ga=2 * a_w, gb=2 * a_w + d_model)
    return main, small, cols, end


def _layer(h, hg, ssq, bsz, t_pad, l, p, w, cols, tiles, next_gain):
    m, d = h.shape
    proj = normed_matmul(hg, ssq, w['in_main_t'], l, F32, tiles['tm'], tiles['tn_in'],
                         w_is_nk=True, relu2=False, name="in_proj")
    small_p = normed_matmul(hg, ssq, w['in_small_t'], l, F32, tiles['tm'], SMALL_W,
                            w_is_nk=True, relu2=False, name="in_proj_small")

    proj3 = proj.reshape(bsz, t_pad, proj.shape[1])
    small3 = small_p.reshape(bsz, t_pad, SMALL_W)
    small_t = jnp.swapaxes(small3[:, :, :2 * HEADS], 1, 2)
    bias_col = jnp.pad(p['b_if'], (0, SMALL_W - 2 * HEADS)).reshape(1, SMALL_W)
    bias_row = jnp.broadcast_to(p['b_if'].reshape(2 * HEADS, 1), (2 * HEADS, MLSTM_CHUNK))
    ya = mlstm_mixer(proj3, small3, small_t, p['conv_qk'], bias_col, bias_row,
                     cols['a_qk'], cols['a_v'], cols['a_og'])

    wg_pad = jnp.zeros((SMALL_W, QK), F32).at[2 * HEADS:2 * HEADS + GATE_RANK].set(p['w_gla_gate'])
    yb = gla_mixer(proj3, small3, wg_pad.astype(BF16), p['b_gla_gate'].reshape(1, QK),
                   p['norm_gla'].reshape(1, VW), cols['b_qk'], cols['b_v'], cols['b_og'])

    merged = merge(ya.reshape(m, VW), yb.reshape(m, VW), w['br_a'], w['br_b'], l, proj,
                   cols['ga'], cols['gb'], tiles['tm'], tiles['tn_merge'])
    h, hg, ssq = residual_matmul(merged, w['out'], l, h, p['norm_mlp'], tiles['tm'],
                                 tiles['tn_out'], name="out_proj")
    u = normed_matmul(hg, ssq, w['up'], l, BF16, tiles['tm'], tiles['tn_up'],
                      w_is_nk=False, relu2=True, name="up_proj")
    return down_proj(u, w['down'], l, h, next_gain, tiles['tm'], tiles['tn_down'], tiles['tk'])


def kernel(x, meta, norm_mix, w_in, conv_qk, b_if, w_gla_gate, b_gla_gate, norm_gla, w_br_a,
           w_br_b, w_out, norm_mlp, w_up, w_down, norm_final):
    bsz, seq, d = x.shape
    depth = w_in.shape[0]
    n = N_META + seq
    t_pad = -(-n // T_ALIGN) * T_ALIGN
    h = jnp.concatenate([jnp.broadcast_to(meta.astype(x.dtype)[None], (bsz, N_META, d)), x,
                         jnp.zeros((bsz, t_pad - n, d), x.dtype)], axis=1)
    h = h.reshape(bsz * t_pad, d)

    main, small, cols, p_in = _projection_layout(d)
    assert w_in.shape[2] == p_in
    w_in_t = jnp.swapaxes(w_in, 1, 2)
    w = dict(in_main_t=pack_rows(w_in_t, main, tile=512),
             in_small_t=gather_rows(w_in_t, small, SMALL_W), br_a=w_br_a.astype(BF16),
             br_b=w_br_b.astype(BF16), out=w_out.astype(BF16), up=w_up.astype(BF16),
             down=w_down.astype(BF16))

    tiles = dict(tm=1280, tn_in=1024, tn_merge=1024, tn_out=512, tn_up=1024, tn_down=1024, tk=2048)
    hg, ssq = gain_stats(h, norm_mix[0])
    for l in range(depth):
        p = dict(conv_qk=conv_qk[l], b_if=b_if[l], w_gla_gate=w_gla_gate[l],
                 b_gla_gate=b_gla_gate[l], norm_gla=norm_gla[l], norm_mlp=norm_mlp[l])
        next_gain = norm_mix[l + 1] if l + 1 < depth else norm_final
        h, hg, ssq = _layer(h, hg, ssq, bsz, t_pad, l, p, w, cols, tiles, next_gain)
    return final_norm(h, norm_final, bsz, N_META, seq)
```

```python
import functools

import jax
import jax.numpy as jnp
from jax import lax
from jax.experimental import pallas as pl
from jax.experimental.pallas import tpu as pltpu

F32 = jnp.float32
BF16 = jnp.bfloat16

N_META = 16
EPS = 1e-6
HEADS = 4
DK = 128
DV = 256
QK = HEADS * DK
VW = HEADS * DV
CONV_W = 4
GATE_RANK = 16
GATE_TAU = 16.0
LANES = 128
SMALL_W = LANES
CONV_HALO = 8

MLSTM_CHUNK = 128
GLA_CHUNK = 64
T_ALIGN = 128

VMEM_LIMIT_BYTES = 60 * 1024 * 1024


def _cparams(sem):
    return pltpu.CompilerParams(dimension_semantics=sem, vmem_limit_bytes=VMEM_LIMIT_BYTES)


def _tile(n, pref):
    if n <= pref:
        return n
    t = (pref // LANES) * LANES
    while t >= LANES:
        if n % t == 0:
            return t
        t -= LANES
    raise ValueError(f"no 128-aligned tile for {n}")


def _sigmoid(x):
    return 1.0 / (1.0 + jnp.exp(-x))


def _log_sigmoid(x):
    return jnp.minimum(x, 0.0) - jnp.log1p(jnp.exp(-jnp.abs(x)))


def _split3(x):
    hi = x.astype(BF16)
    r = x - hi.astype(F32)
    mid = r.astype(BF16)
    lo = (r - mid.astype(F32)).astype(BF16)
    return hi, mid, lo


def _dot(a, b):
    return jnp.dot(a, b, preferred_element_type=F32)


def _dot_nt(a, b):
    return lax.dot_general(a, b, (((1,), (1,)), ((), ())), preferred_element_type=F32)


def _dot_tn(a, b):
    return lax.dot_general(a, b, (((0,), (0,)), ((), ())), preferred_element_type=F32)


def _tri_left(tri, x):
    hi, mid, lo = _split3(x)
    return (_dot(tri, lo) + _dot(tri, mid)) + _dot(tri, hi)


def _tri_right(x, tri):
    hi, mid, lo = _split3(x)
    return (_dot(lo, tri) + _dot(mid, tri)) + _dot(hi, tri)


def _tri(n, lower):
    r = lax.broadcasted_iota(jnp.int32, (n, n), 0)
    c = lax.broadcasted_iota(jnp.int32, (n, n), 1)
    keep = (r >= c) if lower else (r <= c)
    return jnp.where(keep, 1.0, 0.0).astype(BF16)


def _fold_lanes(x):
    out = x[:, 0:LANES]
    for c in range(1, x.shape[1] // LANES):
        out = out + x[:, c * LANES:(c + 1) * LANES]
    return out


def _row_scale(ssq, d):
    return lax.rsqrt(jnp.sum(ssq, axis=1, keepdims=True) * (1.0 / d) + EPS)


def _rmsnorm_kernel(x_ref, g_ref, o_ref):
    x = x_ref[...]
    ms = jnp.mean(x * x, axis=-1, keepdims=True)
    o_ref[...] = (x * lax.rsqrt(ms + EPS) * g_ref[...]).astype(o_ref.dtype)


def _gain_stats_kernel(x_ref, g_ref, hg_ref, sq_ref):
    x = x_ref[...]
    hg_ref[...] = (x * g_ref[...]).astype(hg_ref.dtype)
    sq_ref[...] = _fold_lanes(x * x)


def gain_stats(x, g, rows=256):
    m, d = x.shape
    tr = _tile(m, rows)
    return pl.pallas_call(
        _gain_stats_kernel,
        grid=(m // tr,),
        in_specs=[pl.BlockSpec((tr, d), lambda i: (i, 0)),
                  pl.BlockSpec((1, d), lambda i: (0, 0))],
        out_specs=[pl.BlockSpec((tr, d), lambda i: (i, 0)),
                   pl.BlockSpec((tr, LANES), lambda i: (i, 0))],
        out_shape=[jax.ShapeDtypeStruct((m, d), BF16),
                   jax.ShapeDtypeStruct((m, LANES), F32)],
        compiler_params=_cparams(("parallel",)),
        name="gain_stats",
    )(x, g.reshape(1, d))


def final_norm(h, g, bsz, first_row, n_rows, rows=256):
    m, d = h.shape
    t = m // bsz
    tr = _tile(n_rows, rows)
    assert first_row % 8 == 0 and t % 8 == 0
    window = pl.BlockSpec((pl.Element(tr), pl.Element(d)),
                          lambda b, i: (pl.multiple_of(b * t + first_row + i * tr, 8), 0))
    return pl.pallas_call(
        _rmsnorm_kernel,
        grid=(bsz, n_rows // tr),
        in_specs=[window, pl.BlockSpec((1, d), lambda b, i: (0, 0))],
        out_specs=pl.BlockSpec((None, tr, d), lambda b, i: (b, i, 0)),
        out_shape=jax.ShapeDtypeStruct((bsz, n_rows, d), h.dtype),
        compiler_params=_cparams(("parallel", "parallel")),
        name="final_norm",
    )(h, g.reshape(1, d))


PACK_HALO = 32


def _pack_rows_kernel(a_ref, b_ref, o_ref, *, shifts):
    rows = o_ref.shape[0]
    j = pl.program_id(1)
    for idx, (first, shift) in enumerate(shifts):
        cond = j >= first
        if idx + 1 < len(shifts):
            cond = jnp.logical_and(cond, j < shifts[idx + 1][0])

        @pl.when(cond)
        def _(shift=shift):
            if shift == 0:
                o_ref[...] = a_ref[...].astype(o_ref.dtype)
            else:
                wide = jnp.concatenate([a_ref[shift:rows, :], b_ref[0:shift, :]], axis=0)
                o_ref[...] = wide.astype(o_ref.dtype)


def pack_rows(w, segments, tile):
    layers, r, k = w.shape
    shifts, start = [], 0
    for a, b in segments:
        assert start % tile == 0 and (a - start) % 8 == 0 and 0 <= a - start <= PACK_HALO
        shifts.append((start // tile, a - start))
        start += b - a
    assert start % tile == 0 and tile % PACK_HALO == 0
    return pl.pallas_call(
        functools.partial(_pack_rows_kernel, shifts=tuple(shifts)),
        grid=(layers, start // tile),
        in_specs=[pl.BlockSpec((None, tile, k), lambda l, j: (l, j, 0)),
                  pl.BlockSpec((None, PACK_HALO, k), lambda l, j: (l, (j + 1) * (tile // PACK_HALO), 0))],
        out_specs=pl.BlockSpec((None, tile, k), lambda l, j: (l, j, 0)),
        out_shape=jax.ShapeDtypeStruct((layers, start, k), BF16),
        compiler_params=_cparams(("parallel", "parallel")),
        name="pack_rows",
    )(w, w)


def _gather_rows_kernel(*refs):
    *in_refs, o_ref = refs
    pad = o_ref.shape[0] - 8 * len(in_refs)
    parts = [r[...] for r in in_refs] + [jnp.zeros((pad, o_ref.shape[1]), F32)]
    o_ref[...] = jnp.concatenate(parts, axis=0).astype(o_ref.dtype)


def gather_rows(w, segments, n_out):
    layers, _, k = w.shape
    starts = []
    for a, b in segments:
        assert a % 8 == 0 and (b - a) % 8 == 0
        starts += list(range(a // 8, b // 8))
    return pl.pallas_call(
        _gather_rows_kernel,
        grid=(layers,),
        in_specs=[pl.BlockSpec((None, 8, k), functools.partial(lambda l, s: (l, s, 0), s=s))
                  for s in starts],
        out_specs=pl.BlockSpec((None, n_out, k), lambda l: (l, 0, 0)),
        out_shape=jax.ShapeDtypeStruct((layers, n_out, k), BF16),
        compiler_params=_cparams(("parallel",)),
        name="gather_rows",
    )(*([w] * len(starts)))


def _normed_kernel(x_ref, w_ref, ssq_ref, o_ref, r_ref, *, w_is_nk, relu2):
    @pl.when(pl.program_id(1) == 0)
    def _():
        r_ref[...] = _row_scale(ssq_ref[...], x_ref.shape[1])

    acc = _dot_nt(x_ref[...], w_ref[...]) if w_is_nk else _dot(x_ref[...], w_ref[...])
    acc = acc * r_ref[...]
    if relu2:
        acc = jnp.maximum(acc, 0.0)
        acc = acc * acc
    o_ref[...] = acc.astype(o_ref.dtype)


def normed_matmul(hg, ssq, w, layer, out_dtype, tm, tn, *, w_is_nk, relu2, name):
    m, k = hg.shape
    n = w.shape[1] if w_is_nk else w.shape[2]
    tm, tn = _tile(m, tm), _tile(n, tn)
    if w_is_nk:
        w_spec = pl.BlockSpec((None, tn, k), lambda i, j: (layer, j, 0))
    else:
        w_spec = pl.BlockSpec((None, k, tn), lambda i, j: (layer, 0, j))
    return pl.pallas_call(
        functools.partial(_normed_kernel, w_is_nk=w_is_nk, relu2=relu2),
        grid=(m // tm, n // tn),
        in_specs=[pl.BlockSpec((tm, k), lambda i, j: (i, 0)),
                  w_spec,
                  pl.BlockSpec((tm, ssq.shape[1]), lambda i, j: (i, 0))],
        out_specs=pl.BlockSpec((tm, tn), lambda i, j: (i, j)),
        out_shape=jax.ShapeDtypeStruct((m, n), out_dtype),
        scratch_shapes=[pltpu.VMEM((tm, 1), F32)],
        compiler_params=_cparams(("parallel", "arbitrary")),
        name=name,
    )(hg, w, ssq)


def _emit_stats(h_new, g_ref, hg_ref, sq_ref, first):
    hg_ref[...] = (h_new * g_ref[...]).astype(hg_ref.dtype)
    part = _fold_lanes(h_new * h_new)

    @pl.when(first)
    def _():
        sq_ref[...] = part

    @pl.when(jnp.logical_not(first))
    def _():
        sq_ref[...] += part


def _residual_kernel(x_ref, w_ref, h_ref, g_ref, o_ref, hg_ref, sq_ref):
    h_new = h_ref[...] + _dot(x_ref[...], w_ref[...])
    o_ref[...] = h_new
    _emit_stats(h_new, g_ref, hg_ref, sq_ref, pl.program_id(1) == 0)


def _stats_out(m, n, tm, tn, tile_index, row_index):
    specs = [pl.BlockSpec((tm, tn), tile_index), pl.BlockSpec((tm, tn), tile_index),
             pl.BlockSpec((tm, LANES), row_index)]
    shapes = [jax.ShapeDtypeStruct((m, n), F32), jax.ShapeDtypeStruct((m, n), BF16),
              jax.ShapeDtypeStruct((m, LANES), F32)]
    return specs, shapes


def residual_matmul(x, w, layer, h, gain, tm, tn, name):
    m, k = x.shape
    n = w.shape[2]
    tm, tn = _tile(m, tm), _tile(n, tn)
    out_specs, out_shape = _stats_out(m, n, tm, tn, lambda i, j: (i, j), lambda i, j: (i, 0))
    return pl.pallas_call(
        _residual_kernel,
        grid=(m // tm, n // tn),
        in_specs=[pl.BlockSpec((tm, k), lambda i, j: (i, 0)),
                  pl.BlockSpec((None, k, tn), lambda i, j: (layer, 0, j)),
                  pl.BlockSpec((tm, tn), lambda i, j: (i, j)),
                  pl.BlockSpec((1, tn), lambda i, j: (0, j))],
        out_specs=out_specs,
        out_shape=out_shape,
        compiler_params=_cparams(("parallel", "arbitrary")),
        name=name,
    )(x, w, h, gain.reshape(1, n))


def _down_kernel(u_ref, w_ref, h_ref, g_ref, o_ref, hg_ref, sq_ref, acc_ref):
    kk = pl.program_id(2)
    first_tile = pl.program_id(1) == 0

    @pl.when(kk == 0)
    def _():
        acc_ref[...] = h_ref[...]

    acc_ref[...] += _dot(u_ref[...], w_ref[...])

    @pl.when(kk == pl.num_programs(2) - 1)
    def _():
        h_new = acc_ref[...]
        o_ref[...] = h_new
        _emit_stats(h_new, g_ref, hg_ref, sq_ref, first_tile)


def down_proj(u, w, layer, h, gain, tm, tn, tk):
    m, k = u.shape
    n = w.shape[2]
    tm, tn, tk = _tile(m, tm), _tile(n, tn), _tile(k, tk)
    out_specs, out_shape = _stats_out(m, n, tm, tn, lambda i, j, kk: (i, j),
                                      lambda i, j, kk: (i, 0))
    return pl.pallas_call(
        _down_kernel,
        grid=(m // tm, n // tn, k // tk),
        in_specs=[pl.BlockSpec((tm, tk), lambda i, j, kk: (i, kk)),
                  pl.BlockSpec((None, tk, tn), lambda i, j, kk: (layer, kk, j)),
                  pl.BlockSpec((tm, tn), lambda i, j, kk: (i, j)),
                  pl.BlockSpec((1, tn), lambda i, j, kk: (0, j))],
        out_specs=out_specs,
        out_shape=out_shape,
        scratch_shapes=[pltpu.VMEM((tm, tn), F32)],
        compiler_params=_cparams(("parallel", "arbitrary", "arbitrary")),
        name="down_proj",
    )(u, w, h, gain.reshape(1, n))


def _merge_kernel(ya_ref, yb_ref, wa_ref, wb_ref, ga_ref, gb_ref, o_ref):
    a = _dot(ya_ref[...], wa_ref[...])
    b = _dot(yb_ref[...], wb_ref[...])
    o_ref[...] = (_sigmoid(ga_ref[...]) * a + _sigmoid(gb_ref[...]) * b).astype(o_ref.dtype)


def merge(ya, yb, wa, wb, layer, proj, ga_col, gb_col, tm, tn):
    m, k = ya.shape
    n = wa.shape[2]
    tm, tn = _tile(m, tm), _tile(n, tn)
    ga_blk, gb_blk = ga_col // tn, gb_col // tn
    assert ga_blk * tn == ga_col and gb_blk * tn == gb_col
    return pl.pallas_call(
        _merge_kernel,
        grid=(m // tm, n // tn),
        in_specs=[pl.BlockSpec((tm, k), lambda i, j: (i, 0)),
                  pl.BlockSpec((tm, k), lambda i, j: (i, 0)),
                  pl.BlockSpec((None, k, tn), lambda i, j: (layer, 0, j)),
                  pl.BlockSpec((None, k, tn), lambda i, j: (layer, 0, j)),
                  pl.BlockSpec((tm, tn), lambda i, j: (i, ga_blk + j)),
                  pl.BlockSpec((tm, tn), lambda i, j: (i, gb_blk + j))],
        out_specs=pl.BlockSpec((tm, tn), lambda i, j: (i, j)),
        out_shape=jax.ShapeDtypeStruct((m, n), BF16),
        compiler_params=_cparams(("parallel", "parallel")),
        name="merge",
    )(ya, yb, wa, wb, proj, proj)


def _mlstm_sequence(qk_ref, v_ref, og_ref, gc_ref, gr_ref, convw_ref, bc_ref, br_ref,
                    o_ref, xbuf, c_st, n_st, m_st):
    L = MLSTM_CHUNK

    @pl.when(pl.program_id(0) == 0)
    def _():
        xbuf[0:CONV_HALO, :] = jnp.zeros((CONV_HALO, 2 * QK), F32)
        c_st[...] = jnp.zeros_like(c_st)
        n_st[...] = jnp.zeros_like(n_st)
        m_st[...] = jnp.zeros_like(m_st)

    xbuf[CONV_HALO:CONV_HALO + L, :] = qk_ref[...]
    w = convw_ref[...]
    acc = w[CONV_W - 1:CONV_W, :] * xbuf[CONV_HALO:CONV_HALO + L, :]
    for j in range(CONV_W - 1):
        back = CONV_W - 1 - j
        acc = acc + w[j:j + 1, :] * xbuf[CONV_HALO - back:CONV_HALO - back + L, :]
    xbuf[0:CONV_HALO, :] = xbuf[L:L + CONV_HALO, :]
    qk = acc * _sigmoid(acc)

    g_col = gc_ref[...] + bc_ref[...]
    g_row = gr_ref[...] + br_ref[...]
    b_col = _tri_left(_tri(L, True), _log_sigmoid(g_col))
    b_row = _tri_right(_log_sigmoid(g_row), _tri(L, False))

    row = lax.broadcasted_iota(jnp.int32, (L, L), 0)
    col = lax.broadcasted_iota(jnp.int32, (L, L), 1)
    causal = row >= col

    for h in range(HEADS):
        q_f = qk[:, h * DK:(h + 1) * DK]
        q = q_f.astype(BF16)
        k_f = qk[:, QK + h * DK:QK + (h + 1) * DK] * (DK ** -0.5)
        k = k_f.astype(BF16)
        v_f = v_ref[:, h * DV:(h + 1) * DV]
        v = v_f.astype(BF16)
        li_c = g_col[:, h:h + 1]
        li_r = g_row[h:h + 1, :]
        b_c = b_col[:, HEADS + h:HEADS + h + 1]
        b_r = b_row[HEADS + h:HEADS + h + 1, :]
        m_prev = m_st[h:h + 1, 0:1]
        ct = c_st[h]
        n_row = n_st[h:h + 1, :]

        log_d = jnp.where(causal, b_c - b_r + li_r, -jnp.inf)
        m_t = jnp.maximum(b_c + m_prev, jnp.max(log_d, axis=1, keepdims=True))
        s = _dot_nt(q, k) * jnp.exp(log_d - m_t)
        carry_w = jnp.exp(b_c + m_prev - m_t)
        num = _dot(s.astype(BF16), v) + carry_w * _dot(q, ct.astype(BF16))
        qn = jnp.sum(q_f * n_row, axis=1, keepdims=True)
        den = jnp.sum(s, axis=1, keepdims=True) + carry_w * qn
        hh = num / jnp.maximum(jnp.abs(den), jnp.exp(-m_t))
        og = og_ref[:, h * DV:(h + 1) * DV]
        o_ref[:, h * DV:(h + 1) * DV] = (hh * _sigmoid(og)).astype(o_ref.dtype)

        g_tot = b_c[L - 1:L, :]
        log_w = g_tot - b_c + li_c
        m_new = jnp.maximum(g_tot + m_prev, jnp.max(log_w, axis=0, keepdims=True))
        wgt = jnp.exp(log_w - m_new)
        decay = jnp.exp(g_tot + m_prev - m_new)
        kv = _dot_tn(k, (wgt * v_f).astype(BF16))
        c_st[h] = decay * ct + kv
        n_st[h:h + 1, :] = decay * n_row + jnp.sum(wgt * k_f, axis=0, keepdims=True)
        m_st[h:h + 1, :] = jnp.broadcast_to(m_new, (1, m_st.shape[1]))


def _ride_along_specs(src, layer, steps):
    _, r, c = src.shape
    rb = next(rb for rb in range(16, r + 1, 16) if r % rb == 0 and r // rb <= steps)
    last = r // rb - 1
    in_spec = pl.BlockSpec((None, rb, c), lambda s: (layer, jnp.minimum(s, last), 0))
    out_spec = pl.BlockSpec((rb, c), lambda s: (jnp.minimum(s, last), 0))
    return in_spec, out_spec, jax.ShapeDtypeStruct((r, c), BF16)


def _mlstm_kernel(qk_ref, v_ref, og_ref, gc_ref, gr_ref, convw_ref, bc_ref, br_ref, wsrc_ref,
                  o_ref, wdst_ref, xbuf, c_st, n_st, m_st):
    wdst_ref[...] = wsrc_ref[...].astype(wdst_ref.dtype)
    for b in range(qk_ref.shape[0]):
        _mlstm_sequence(qk_ref.at[b], v_ref.at[b], og_ref.at[b], gc_ref.at[b], gr_ref.at[b],
                        convw_ref, bc_ref, br_ref, o_ref.at[b], xbuf.at[b], c_st.at[b],
                        n_st.at[b], m_st.at[b])


def mlstm_mixer(proj, small, small_t, conv_w, bias_col, bias_row, qk_col, v_col, og_col,
                cast_src, layer):
    bsz, t, _ = proj.shape
    L = MLSTM_CHUNK
    assert t % L == 0
    qk_blk, v_blk, og_blk = qk_col // (2 * QK), v_col // VW, og_col // VW
    cast_in, cast_out, cast_shape = _ride_along_specs(cast_src, layer, t // L)
    return pl.pallas_call(
        _mlstm_kernel,
        grid=(t // L,),
        in_specs=[pl.BlockSpec((bsz, L, 2 * QK), lambda c: (0, c, qk_blk)),
                  pl.BlockSpec((bsz, L, VW), lambda c: (0, c, v_blk)),
                  pl.BlockSpec((bsz, L, VW), lambda c: (0, c, og_blk)),
                  pl.BlockSpec((bsz, L, SMALL_W), lambda c: (0, c, 0)),
                  pl.BlockSpec((bsz, 8, L), lambda c: (0, 0, c)),
                  pl.BlockSpec((CONV_W, 2 * QK), lambda c: (0, 0)),
                  pl.BlockSpec((1, SMALL_W), lambda c: (0, 0)),
                  pl.BlockSpec((8, L), lambda c: (0, 0)),
                  cast_in],
        out_specs=[pl.BlockSpec((bsz, L, VW), lambda c: (0, c, 0)), cast_out],
        out_shape=[jax.ShapeDtypeStruct((bsz, t, VW), BF16), cast_shape],
        scratch_shapes=[pltpu.VMEM((bsz, L + CONV_HALO, 2 * QK), F32),
                        pltpu.VMEM((bsz, HEADS, DK, DV), F32),
                        pltpu.VMEM((bsz, 8, DK), F32),
                        pltpu.VMEM((bsz, 8, 128), F32)],
        compiler_params=_cparams(("arbitrary",)),
        name="mlstm",
    )(proj, proj, proj, small, small_t, conv_w, bias_col, bias_row, cast_src)


def _half_block_ref(x, w):
    n_rows, n = x.shape
    if 2 * w >= 8:
        x3 = x.reshape(n_rows // (2 * w), 2 * w, n)
        return jnp.broadcast_to(x3[:, w - 1:w, :], x3.shape).reshape(n_rows, n)
    x3 = x.reshape(n_rows // 8, 8, n)
    sub = lax.broadcasted_iota(jnp.int32, x3.shape, 1)
    groups = 8 // (2 * w)
    pick = lambda g: jnp.broadcast_to(x3[:, g * 2 * w + w - 1:g * 2 * w + w, :], x3.shape)
    out = pick(groups - 1)
    for g in reversed(range(groups - 1)):
        out = jnp.where(sub < (g + 1) * 2 * w, pick(g), out)
    return out.reshape(n_rows, n)


def _gla_sequence(qk_ref, v_ref, og_ref, gc_ref, wg_ref, bg_ref, gn_ref, o_ref, s_st):
    L = GLA_CHUNK

    @pl.when(pl.program_id(0) == 0)
    def _():
        s_st[...] = jnp.zeros_like(s_st)

    z = _dot(gc_ref[...].astype(BF16), wg_ref[...]) + bg_ref[...]
    log_a = _log_sigmoid(z) * (1.0 / GATE_TAU)
    cum = _tri_left(_tri(L, True), log_a)

    row = lax.broadcasted_iota(jnp.int32, (L, L), 0)
    col = lax.broadcasted_iota(jnp.int32, (L, L), 1)
    level = jnp.where(row > col, 31 - lax.clz(row ^ col), -1)
    diag = row == col
    eye = (lax.broadcasted_iota(jnp.int32, (DK, DK), 0)
           == lax.broadcasted_iota(jnp.int32, (DK, DK), 1))
    n_levels = L.bit_length() - 1

    for h in range(HEADS):
        q_f = qk_ref[:, h * DK:(h + 1) * DK] * (DK ** -0.5)
        k_f = qk_ref[:, QK + h * DK:QK + (h + 1) * DK]
        v = v_ref[:, h * DV:(h + 1) * DV].astype(BF16)
        cum_h = cum[:, h * DK:(h + 1) * DK]

        att = jnp.where(diag, jnp.sum(q_f * k_f, axis=1, keepdims=True), 0.0)
        for lw in range(n_levels):
            f = jnp.exp(-jnp.abs(cum_h - _half_block_ref(cum_h, 1 << lw)))
            p = _dot_nt((q_f * f).astype(BF16), (k_f * f).astype(BF16))
            att = att + jnp.where(level == lw, p, 0.0)

        st = s_st[h]
        o = _dot(att.astype(BF16), v) + _dot((q_f * jnp.exp(cum_h)).astype(BF16), st.astype(BF16))
        last = cum_h[L - 1:L, :]
        k_dec = (k_f * jnp.exp(last - cum_h)).astype(BF16)
        kv = _dot_tn(k_dec, v)
        decay_col = jnp.sum(jnp.where(eye, jnp.exp(last), 0.0), axis=1, keepdims=True)
        s_st[h] = decay_col * st + kv

        o = o * lax.rsqrt(jnp.mean(o * o, axis=1, keepdims=True) + EPS)
        og = og_ref[:, h * DV:(h + 1) * DV]
        y = o * gn_ref[:, h * DV:(h + 1) * DV] * (og * _sigmoid(og))
        o_ref[:, h * DV:(h + 1) * DV] = y.astype(o_ref.dtype)


def _gla_kernel(qk_ref, v_ref, og_ref, gc_ref, wg_ref, bg_ref, gn_ref, wsrc_ref,
                o_ref, wdst_ref, s_st):
    wdst_ref[...] = wsrc_ref[...].astype(wdst_ref.dtype)
    for b in range(qk_ref.shape[0]):
        _gla_sequence(qk_ref.at[b], v_ref.at[b], og_ref.at[b], gc_ref.at[b], wg_ref, bg_ref,
                      gn_ref, o_ref.at[b], s_st.at[b])


def gla_mixer(proj, small, wg_pad, bg, gn, qk_col, v_col, og_col, cast_src, layer):
    bsz, t, _ = proj.shape
    L = GLA_CHUNK
    assert t % L == 0
    qk_blk, v_blk, og_blk = qk_col // (2 * QK), v_col // VW, og_col // VW
    cast_in, cast_out, cast_shape = _ride_along_specs(cast_src, layer, t // L)
    return pl.pallas_call(
        _gla_kernel,
        grid=(t // L,),
        in_specs=[pl.BlockSpec((bsz, L, 2 * QK), lambda c: (0, c, qk_blk)),
                  pl.BlockSpec((bsz, L, VW), lambda c: (0, c, v_blk)),
                  pl.BlockSpec((bsz, L, VW), lambda c: (0, c, og_blk)),
                  pl.BlockSpec((bsz, L, SMALL_W), lambda c: (0, c, 0)),
                  pl.BlockSpec((SMALL_W, QK), lambda c: (0, 0)),
                  pl.BlockSpec((1, QK), lambda c: (0, 0)),
                  pl.BlockSpec((1, VW), lambda c: (0, 0)),
                  cast_in],
        out_specs=[pl.BlockSpec((bsz, L, VW), lambda c: (0, c, 0)), cast_out],
        out_shape=[jax.ShapeDtypeStruct((bsz, t, VW), BF16), cast_shape],
        scratch_shapes=[pltpu.VMEM((bsz, HEADS, DK, DV), F32)],
        compiler_params=_cparams(("arbitrary",)),
        name="gla",
    )(proj, proj, proj, small, wg_pad, bg, gn, cast_src)


def _projection_layout(d_model):
    a_w = 2 * QK + 2 * VW
    src_a = 0
    src_if = a_w
    src_b = src_if + 2 * HEADS
    src_z = src_b + a_w
    src_g = src_z + GATE_RANK
    end = src_g + 2 * d_model
    main = [(src_a, src_if), (src_b, src_z), (src_g, end)]
    small = [(src_if, src_b), (src_z, src_g)]
    cols = dict(a_qk=0, a_v=2 * QK, a_og=2 * QK + VW, b_qk=a_w, b_v=a_w + 2 * QK,
                b_og=a_w + 2 * QK + VW, ga=2 * a_w, gb=2 * a_w + d_model)
    return main, small, cols, end


def _layer(h, hg, ssq, bsz, t_pad, l, p, w, cols, tiles, next_gain):
    m, d = h.shape
    proj = normed_matmul(hg, ssq, w['in_main_t'], l, F32, tiles['tm'], tiles['tn_in'],
                         w_is_nk=True, relu2=False, name="in_proj")
    small_p = normed_matmul(hg, ssq, w['in_small_t'], l, F32, tiles['tm'], SMALL_W,
                            w_is_nk=True, relu2=False, name="in_proj_small")

    proj3 = proj.reshape(bsz, t_pad, proj.shape[1])
    small3 = small_p.reshape(bsz, t_pad, SMALL_W)
    small_t = jnp.swapaxes(small3[:, :, :2 * HEADS], 1, 2)
    bias_col = jnp.pad(p['b_if'], (0, SMALL_W - 2 * HEADS)).reshape(1, SMALL_W)
    bias_row = jnp.broadcast_to(p['b_if'].reshape(2 * HEADS, 1), (2 * HEADS, MLSTM_CHUNK))
    ya, w_down = mlstm_mixer(proj3, small3, small_t, p['conv_qk'], bias_col, bias_row,
                             cols['a_qk'], cols['a_v'], cols['a_og'], w['down_f32'], l)

    wg_pad = jnp.zeros((SMALL_W, QK), F32).at[2 * HEADS:2 * HEADS + GATE_RANK].set(p['w_gla_gate'])
    yb, w_up = gla_mixer(proj3, small3, wg_pad.astype(BF16), p['b_gla_gate'].reshape(1, QK),
                         p['norm_gla'].reshape(1, VW), cols['b_qk'], cols['b_v'], cols['b_og'],
                         w['up_f32'], l)

    merged = merge(ya.reshape(m, VW), yb.reshape(m, VW), w['br_a'], w['br_b'], l, proj,
                   cols['ga'], cols['gb'], tiles['tm'], tiles['tn_merge'])
    h, hg, ssq = residual_matmul(merged, w['out'], l, h, p['norm_mlp'], tiles['tm'],
                                 tiles['tn_out'], name="out_proj")
    u = normed_matmul(hg, ssq, w_up[None], 0, BF16, tiles['tm'], tiles['tn_up'],
                      w_is_nk=False, relu2=True, name="up_proj")
    return down_proj(u, w_down[None], 0, h, next_gain, tiles['tm'], tiles['tn_down'], tiles['tk'])


def kernel(x, meta, norm_mix, w_in, conv_qk, b_if, w_gla_gate, b_gla_gate, norm_gla, w_br_a,
           w_br_b, w_out, norm_mlp, w_up, w_down, norm_final):
    bsz, seq, d = x.shape
    depth = w_in.shape[0]
    n = N_META + seq
    t_pad = -(-n // T_ALIGN) * T_ALIGN
    h = jnp.concatenate([jnp.broadcast_to(meta.astype(x.dtype)[None], (bsz, N_META, d)), x,
                         jnp.zeros((bsz, t_pad - n, d), x.dtype)], axis=1)
    h = h.reshape(bsz * t_pad, d)

    main, small, cols, p_in = _projection_layout(d)
    assert w_in.shape[2] == p_in
    w_in_t = jnp.swapaxes(w_in, 1, 2)
    w = dict(in_main_t=pack_rows(w_in_t, main, tile=512),
             in_small_t=gather_rows(w_in_t, small, SMALL_W), br_a=w_br_a.astype(BF16),
             br_b=w_br_b.astype(BF16), out=w_out.astype(BF16), up_f32=w_up, down_f32=w_down)

    tiles = dict(tm=1280, tn_in=1024, tn_merge=1024, tn_out=512, tn_up=1024, tn_down=1024, tk=2048)
    hg, ssq = gain_stats(h, norm_mix[0])
    for l in range(depth):
        p = dict(conv_qk=conv_qk[l], b_if=b_if[l], w_gla_gate=w_gla_gate[l],
                 b_gla_gate=b_gla_gate[l], norm_gla=norm_gla[l], norm_mlp=norm_mlp[l])
        next_gain = norm_mix[l + 1] if l + 1 < depth else norm_final
        h, hg, ssq = _layer(h, hg, ssq, bsz, t_pad, l, p, w, cols, tiles, next_gain)
    return final_norm(h, norm_final, bsz, N_META, seq)
```

```python
import functools

import jax
import jax.numpy as jnp
from jax import lax
from jax.experimental import pallas as pl
from jax.experimental.pallas import tpu as pltpu

F32 = jnp.float32
BF16 = jnp.bfloat16

N_META = 16
EPS = 1e-6
HEADS = 4
DK = 128
DV = 256
QK = HEADS * DK
VW = HEADS * DV
CONV_W = 4
GATE_RANK = 16
GATE_TAU = 16.0
LANES = 128
SMALL_W = LANES
CONV_HALO = 8

MLSTM_CHUNK = 128
GLA_CHUNK = 64
T_ALIGN = 128

VMEM_LIMIT_BYTES = 60 * 1024 * 1024


def _cparams(sem):
    return pltpu.CompilerParams(dimension_semantics=sem, vmem_limit_bytes=VMEM_LIMIT_BYTES)


def _tile(n, pref):
    if n <= pref:
        return n
    t = (pref // LANES) * LANES
    while t >= LANES:
        if n % t == 0:
            return t
        t -= LANES
    raise ValueError(f"no 128-aligned tile for {n}")


def _sigmoid(x):
    return 0.5 * jnp.tanh(0.5 * x) + 0.5


def _log_sigmoid(x):
    return jnp.minimum(x, 0.0) - jnp.log1p(jnp.exp(-jnp.abs(x)))


def _split3(x):
    hi = x.astype(BF16)
    r = x - hi.astype(F32)
    mid = r.astype(BF16)
    lo = (r - mid.astype(F32)).astype(BF16)
    return hi, mid, lo


def _dot(a, b):
    return jnp.dot(a, b, preferred_element_type=F32)


def _dot_nt(a, b):
    return lax.dot_general(a, b, (((1,), (1,)), ((), ())), preferred_element_type=F32)


def _dot_tn(a, b):
    return lax.dot_general(a, b, (((0,), (0,)), ((), ())), preferred_element_type=F32)


def _tri_left(tri, x):
    hi, mid, lo = _split3(x)
    return (_dot(tri, lo) + _dot(tri, mid)) + _dot(tri, hi)


def _tri_right(x, tri):
    hi, mid, lo = _split3(x)
    return (_dot(lo, tri) + _dot(mid, tri)) + _dot(hi, tri)


def _tri(n, lower):
    r = lax.broadcasted_iota(jnp.int32, (n, n), 0)
    c = lax.broadcasted_iota(jnp.int32, (n, n), 1)
    keep = (r >= c) if lower else (r <= c)
    return jnp.where(keep, 1.0, 0.0).astype(BF16)


def _fold_lanes(x):
    out = x[:, 0:LANES]
    for c in range(1, x.shape[1] // LANES):
        out = out + x[:, c * LANES:(c + 1) * LANES]
    return out


def _row_scale(ssq, d):
    return lax.rsqrt(jnp.sum(ssq, axis=1, keepdims=True) * (1.0 / d) + EPS)


def _rmsnorm_kernel(x_ref, g_ref, o_ref):
    x = x_ref[...]
    ms = jnp.mean(x * x, axis=-1, keepdims=True)
    o_ref[...] = (x * lax.rsqrt(ms + EPS) * g_ref[...]).astype(o_ref.dtype)


def _gain_stats_kernel(x_ref, g_ref, hg_ref, sq_ref):
    x = x_ref[...]
    hg_ref[...] = (x * g_ref[...]).astype(hg_ref.dtype)
    sq_ref[...] = _fold_lanes(x * x)


def gain_stats(x, g, rows=256):
    m, d = x.shape
    tr = _tile(m, rows)
    return pl.pallas_call(
        _gain_stats_kernel,
        grid=(m // tr,),
        in_specs=[pl.BlockSpec((tr, d), lambda i: (i, 0)),
                  pl.BlockSpec((1, d), lambda i: (0, 0))],
        out_specs=[pl.BlockSpec((tr, d), lambda i: (i, 0)),
                   pl.BlockSpec((tr, LANES), lambda i: (i, 0))],
        out_shape=[jax.ShapeDtypeStruct((m, d), BF16),
                   jax.ShapeDtypeStruct((m, LANES), F32)],
        compiler_params=_cparams(("parallel",)),
        name="gain_stats",
    )(x, g.reshape(1, d))


def final_norm(h, g, bsz, first_row, n_rows, rows=256):
    m, d = h.shape
    t = m // bsz
    tr = _tile(n_rows, rows)
    assert first_row % 8 == 0 and t % 8 == 0
    window = pl.BlockSpec((pl.Element(tr), pl.Element(d)),
                          lambda b, i: (pl.multiple_of(b * t + first_row + i * tr, 8), 0))
    return pl.pallas_call(
        _rmsnorm_kernel,
        grid=(bsz, n_rows // tr),
        in_specs=[window, pl.BlockSpec((1, d), lambda b, i: (0, 0))],
        out_specs=pl.BlockSpec((None, tr, d), lambda b, i: (b, i, 0)),
        out_shape=jax.ShapeDtypeStruct((bsz, n_rows, d), h.dtype),
        compiler_params=_cparams(("parallel", "parallel")),
        name="final_norm",
    )(h, g.reshape(1, d))


PACK_HALO = 32


def _pack_rows_kernel(a_ref, b_ref, o_ref, *, shifts):
    rows = o_ref.shape[0]
    j = pl.program_id(1)
    for idx, (first, shift) in enumerate(shifts):
        cond = j >= first
        if idx + 1 < len(shifts):
            cond = jnp.logical_and(cond, j < shifts[idx + 1][0])

        @pl.when(cond)
        def _(shift=shift):
            if shift == 0:
                o_ref[...] = a_ref[...].astype(o_ref.dtype)
            else:
                wide = jnp.concatenate([a_ref[shift:rows, :], b_ref[0:shift, :]], axis=0)
                o_ref[...] = wide.astype(o_ref.dtype)


def pack_rows(w, segments, tile):
    layers, r, k = w.shape
    shifts, start = [], 0
    for a, b in segments:
        assert start % tile == 0 and (a - start) % 8 == 0 and 0 <= a - start <= PACK_HALO
        shifts.append((start // tile, a - start))
        start += b - a
    assert start % tile == 0 and tile % PACK_HALO == 0
    return pl.pallas_call(
        functools.partial(_pack_rows_kernel, shifts=tuple(shifts)),
        grid=(layers, start // tile),
        in_specs=[pl.BlockSpec((None, tile, k), lambda l, j: (l, j, 0)),
                  pl.BlockSpec((None, PACK_HALO, k), lambda l, j: (l, (j + 1) * (tile // PACK_HALO), 0))],
        out_specs=pl.BlockSpec((None, tile, k), lambda l, j: (l, j, 0)),
        out_shape=jax.ShapeDtypeStruct((layers, start, k), BF16),
        compiler_params=_cparams(("parallel", "parallel")),
        name="pack_rows",
    )(w, w)


def _gather_rows_kernel(*refs):
    *in_refs, o_ref = refs
    pad = o_ref.shape[0] - 8 * len(in_refs)
    parts = [r[...] for r in in_refs] + [jnp.zeros((pad, o_ref.shape[1]), F32)]
    o_ref[...] = jnp.concatenate(parts, axis=0).astype(o_ref.dtype)


def gather_rows(w, segments, n_out):
    layers, _, k = w.shape
    starts = []
    for a, b in segments:
        assert a % 8 == 0 and (b - a) % 8 == 0
        starts += list(range(a // 8, b // 8))
    return pl.pallas_call(
        _gather_rows_kernel,
        grid=(layers,),
        in_specs=[pl.BlockSpec((None, 8, k), functools.partial(lambda l, s: (l, s, 0), s=s))
                  for s in starts],
        out_specs=pl.BlockSpec((None, n_out, k), lambda l: (l, 0, 0)),
        out_shape=jax.ShapeDtypeStruct((layers, n_out, k), BF16),
        compiler_params=_cparams(("parallel",)),
        name="gather_rows",
    )(*([w] * len(starts)))


def _normed_kernel(x_ref, w_ref, ssq_ref, o_ref, r_ref, *, w_is_nk, relu2):
    @pl.when(pl.program_id(1) == 0)
    def _():
        r_ref[...] = _row_scale(ssq_ref[...], x_ref.shape[1])

    acc = _dot_nt(x_ref[...], w_ref[...]) if w_is_nk else _dot(x_ref[...], w_ref[...])
    acc = acc * r_ref[...]
    if relu2:
        acc = jnp.maximum(acc, 0.0)
        acc = acc * acc
    o_ref[...] = acc.astype(o_ref.dtype)


def normed_matmul(hg, ssq, w, layer, out_dtype, tm, tn, *, w_is_nk, relu2, name):
    m, k = hg.shape
    n = w.shape[1] if w_is_nk else w.shape[2]
    tm, tn = _tile(m, tm), _tile(n, tn)
    if w_is_nk:
        w_spec = pl.BlockSpec((None, tn, k), lambda i, j: (layer, j, 0))
    else:
        w_spec = pl.BlockSpec((None, k, tn), lambda i, j: (layer, 0, j))
    return pl.pallas_call(
        functools.partial(_normed_kernel, w_is_nk=w_is_nk, relu2=relu2),
        grid=(m // tm, n // tn),
        in_specs=[pl.BlockSpec((tm, k), lambda i, j: (i, 0)),
                  w_spec,
                  pl.BlockSpec((tm, ssq.shape[1]), lambda i, j: (i, 0))],
        out_specs=pl.BlockSpec((tm, tn), lambda i, j: (i, j)),
        out_shape=jax.ShapeDtypeStruct((m, n), out_dtype),
        scratch_shapes=[pltpu.VMEM((tm, 1), F32)],
        compiler_params=_cparams(("parallel", "arbitrary")),
        name=name,
    )(hg, w, ssq)


def _emit_stats(h_new, g_ref, hg_ref, sq_ref, first):
    hg_ref[...] = (h_new * g_ref[...]).astype(hg_ref.dtype)
    part = _fold_lanes(h_new * h_new)

    @pl.when(first)
    def _():
        sq_ref[...] = part

    @pl.when(jnp.logical_not(first))
    def _():
        sq_ref[...] += part


def _residual_kernel(x_ref, w_ref, h_ref, g_ref, o_ref, hg_ref, sq_ref):
    h_new = h_ref[...] + _dot(x_ref[...], w_ref[...])
    o_ref[...] = h_new
    _emit_stats(h_new, g_ref, hg_ref, sq_ref, pl.program_id(1) == 0)


def _stats_out(m, n, tm, tn, tile_index, row_index):
    specs = [pl.BlockSpec((tm, tn), tile_index), pl.BlockSpec((tm, tn), tile_index),
             pl.BlockSpec((tm, LANES), row_index)]
    shapes = [jax.ShapeDtypeStruct((m, n), F32), jax.ShapeDtypeStruct((m, n), BF16),
              jax.ShapeDtypeStruct((m, LANES), F32)]
    return specs, shapes


def residual_matmul(x, w, layer, h, gain, tm, tn, name):
    m, k = x.shape
    n = w.shape[2]
    tm, tn = _tile(m, tm), _tile(n, tn)
    out_specs, out_shape = _stats_out(m, n, tm, tn, lambda i, j: (i, j), lambda i, j: (i, 0))
    return pl.pallas_call(
        _residual_kernel,
        grid=(m // tm, n // tn),
        in_specs=[pl.BlockSpec((tm, k), lambda i, j: (i, 0)),
                  pl.BlockSpec((None, k, tn), lambda i, j: (layer, 0, j)),
                  pl.BlockSpec((tm, tn), lambda i, j: (i, j)),
                  pl.BlockSpec((1, tn), lambda i, j: (0, j))],
        out_specs=out_specs,
        out_shape=out_shape,
        compiler_params=_cparams(("parallel", "arbitrary")),
        name=name,
    )(x, w, h, gain.reshape(1, n))


def _down_kernel(u_ref, w_ref, h_ref, g_ref, o_ref, hg_ref, sq_ref, acc_ref):
    kk = pl.program_id(2)
    last = pl.num_programs(2) - 1
    first_tile = pl.program_id(1) == 0

    @pl.when(kk == 0)
    def _():
        acc_ref[...] = h_ref[...] + _dot(u_ref[...], w_ref[...])

    @pl.when(jnp.logical_and(kk > 0, kk < last))
    def _():
        acc_ref[...] += _dot(u_ref[...], w_ref[...])

    @pl.when(kk == last)
    def _():
        h_new = acc_ref[...] + _dot(u_ref[...], w_ref[...])
        o_ref[...] = h_new
        _emit_stats(h_new, g_ref, hg_ref, sq_ref, first_tile)


def down_proj(u, w, layer, h, gain, tm, tn, tk):
    m, k = u.shape
    n = w.shape[2]
    tm, tn, tk = _tile(m, tm), _tile(n, tn), _tile(k, tk)
    assert k // tk >= 2
    out_specs, out_shape = _stats_out(m, n, tm, tn, lambda i, j, kk: (i, j),
                                      lambda i, j, kk: (i, 0))
    return pl.pallas_call(
        _down_kernel,
        grid=(m // tm, n // tn, k // tk),
        in_specs=[pl.BlockSpec((tm, tk), lambda i, j, kk: (i, kk)),
                  pl.BlockSpec((None, tk, tn), lambda i, j, kk: (layer, kk, j)),
                  pl.BlockSpec((tm, tn), lambda i, j, kk: (i, j)),
                  pl.BlockSpec((1, tn), lambda i, j, kk: (0, j))],
        out_specs=out_specs,
        out_shape=out_shape,
        scratch_shapes=[pltpu.VMEM((tm, tn), F32)],
        compiler_params=_cparams(("parallel", "arbitrary", "arbitrary")),
        name="down_proj",
    )(u, w, h, gain.reshape(1, n))


def _merge_kernel(ya_ref, yb_ref, wa_ref, wb_ref, ga_ref, gb_ref, o_ref):
    a = _dot(ya_ref[...], wa_ref[...])
    b = _dot(yb_ref[...], wb_ref[...])
    ga = _sigmoid(ga_ref[...].astype(F32))
    gb = _sigmoid(gb_ref[...].astype(F32))
    o_ref[...] = (ga * a + gb * b).astype(o_ref.dtype)


def merge(ya, yb, wa, wb, layer, proj, ga_col, gb_col, tm, tn):
    m, k = ya.shape
    n = wa.shape[2]
    tm, tn = _tile(m, tm), _tile(n, tn)
    ga_blk, gb_blk = ga_col // tn, gb_col // tn
    assert ga_blk * tn == ga_col and gb_blk * tn == gb_col
    return pl.pallas_call(
        _merge_kernel,
        grid=(m // tm, n // tn),
        in_specs=[pl.BlockSpec((tm, k), lambda i, j: (i, 0)),
                  pl.BlockSpec((tm, k), lambda i, j: (i, 0)),
                  pl.BlockSpec((None, k, tn), lambda i, j: (layer, 0, j)),
                  pl.BlockSpec((None, k, tn), lambda i, j: (layer, 0, j)),
                  pl.BlockSpec((tm, tn), lambda i, j: (i, ga_blk + j)),
                  pl.BlockSpec((tm, tn), lambda i, j: (i, gb_blk + j))],
        out_specs=pl.BlockSpec((tm, tn), lambda i, j: (i, j)),
        out_shape=jax.ShapeDtypeStruct((m, n), BF16),
        compiler_params=_cparams(("parallel", "parallel")),
        name="merge",
    )(ya, yb, wa, wb, proj, proj)


def _mlstm_sequence(qk_ref, v_ref, og_ref, gc_ref, gr_ref, convw_ref, bc_ref, br_ref,
                    o_ref, xbuf, c_st, n_st, m_st):
    L = MLSTM_CHUNK

    @pl.when(pl.program_id(0) == 0)
    def _():
        xbuf[0:CONV_HALO, :] = jnp.zeros((CONV_HALO, 2 * QK), F32)
        c_st[...] = jnp.zeros_like(c_st)
        n_st[...] = jnp.zeros_like(n_st)
        m_st[...] = jnp.zeros_like(m_st)

    xbuf[CONV_HALO:CONV_HALO + L, :] = qk_ref[...].astype(F32)
    w = convw_ref[...]
    acc = w[CONV_W - 1:CONV_W, :] * xbuf[CONV_HALO:CONV_HALO + L, :]
    for j in range(CONV_W - 1):
        back = CONV_W - 1 - j
        acc = acc + w[j:j + 1, :] * xbuf[CONV_HALO - back:CONV_HALO - back + L, :]
    xbuf[0:CONV_HALO, :] = xbuf[L:L + CONV_HALO, :]
    qk = acc * _sigmoid(acc)

    g_col = gc_ref[...] + bc_ref[...]
    g_row = gr_ref[...] + br_ref[...]
    b_col = _tri_left(_tri(L, True), _log_sigmoid(g_col))
    b_row = _tri_right(_log_sigmoid(g_row), _tri(L, False))

    row = lax.broadcasted_iota(jnp.int32, (L, L), 0)
    col = lax.broadcasted_iota(jnp.int32, (L, L), 1)
    causal = row >= col

    for h in range(HEADS):
        q_f = qk[:, h * DK:(h + 1) * DK]
        q = q_f.astype(BF16)
        k_f = qk[:, QK + h * DK:QK + (h + 1) * DK] * (DK ** -0.5)
        k = k_f.astype(BF16)
        v = v_ref[:, h * DV:(h + 1) * DV].astype(BF16)
        v_f = v.astype(F32)
        li_c = g_col[:, h:h + 1]
        li_r = g_row[h:h + 1, :]
        b_c = b_col[:, HEADS + h:HEADS + h + 1]
        b_r = b_row[HEADS + h:HEADS + h + 1, :]
        m_prev = m_st[h:h + 1, 0:1]
        ct = c_st[h]
        n_row = n_st[h:h + 1, :]

        log_d = jnp.where(causal, b_c - b_r + li_r, -jnp.inf)
        m_t = jnp.maximum(b_c + m_prev, jnp.max(log_d, axis=1, keepdims=True))
        s = _dot_nt(q, k) * jnp.exp(log_d - m_t)
        carry_w = jnp.exp(b_c + m_prev - m_t)
        num = _dot(s.astype(BF16), v) + carry_w * _dot(q, ct.astype(BF16))
        qn = jnp.sum(q_f * n_row, axis=1, keepdims=True)
        den = jnp.sum(s, axis=1, keepdims=True) + carry_w * qn
        hh = num / jnp.maximum(jnp.abs(den), jnp.exp(-m_t))
        og = og_ref[:, h * DV:(h + 1) * DV].astype(F32)
        o_ref[:, h * DV:(h + 1) * DV] = (hh * _sigmoid(og)).astype(o_ref.dtype)

        g_tot = b_c[L - 1:L, :]
        log_w = g_tot - b_c + li_c
        m_new = jnp.maximum(g_tot + m_prev, jnp.max(log_w, axis=0, keepdims=True))
        wgt = jnp.exp(log_w - m_new)
        decay = jnp.exp(g_tot + m_prev - m_new)
        kv = _dot_tn(k, (wgt * v_f).astype(BF16))
        c_st[h] = decay * ct + kv
        n_st[h:h + 1, :] = decay * n_row + jnp.sum(wgt * k_f, axis=0, keepdims=True)
        m_st[h:h + 1, :] = jnp.broadcast_to(m_new, (1, m_st.shape[1]))


def _ride_along_specs(src, layer, steps):
    _, r, c = src.shape
    rb = next(rb for rb in range(16, r + 1, 16) if r % rb == 0 and r // rb <= steps)
    last = r // rb - 1
    in_spec = pl.BlockSpec((None, rb, c), lambda s: (layer, jnp.minimum(s, last), 0))
    out_spec = pl.BlockSpec((rb, c), lambda s: (jnp.minimum(s, last), 0))
    return in_spec, out_spec, jax.ShapeDtypeStruct((r, c), BF16)


def _mlstm_kernel(qk_ref, v_ref, og_ref, gc_ref, gr_ref, convw_ref, bc_ref, br_ref, wsrc_ref,
                  o_ref, wdst_ref, xbuf, c_st, n_st, m_st):
    wdst_ref[...] = wsrc_ref[...].astype(wdst_ref.dtype)
    for b in range(qk_ref.shape[0]):
        _mlstm_sequence(qk_ref.at[b], v_ref.at[b], og_ref.at[b], gc_ref.at[b], gr_ref.at[b],
                        convw_ref, bc_ref, br_ref, o_ref.at[b], xbuf.at[b], c_st.at[b],
                        n_st.at[b], m_st.at[b])


def mlstm_mixer(proj, small, small_t, conv_w, bias_col, bias_row, qk_col, v_col, og_col,
                cast_src, layer):
    bsz, t, _ = proj.shape
    L = MLSTM_CHUNK
    assert t % L == 0
    qk_blk, v_blk, og_blk = qk_col // (2 * QK), v_col // VW, og_col // VW
    cast_in, cast_out, cast_shape = _ride_along_specs(cast_src, layer, t // L)
    return pl.pallas_call(
        _mlstm_kernel,
        grid=(t // L,),
        in_specs=[pl.BlockSpec((bsz, L, 2 * QK), lambda c: (0, c, qk_blk)),
                  pl.BlockSpec((bsz, L, VW), lambda c: (0, c, v_blk)),
                  pl.BlockSpec((bsz, L, VW), lambda c: (0, c, og_blk)),
                  pl.BlockSpec((bsz, L, SMALL_W), lambda c: (0, c, 0)),
                  pl.BlockSpec((bsz, 8, L), lambda c: (0, 0, c)),
                  pl.BlockSpec((CONV_W, 2 * QK), lambda c: (0, 0)),
                  pl.BlockSpec((1, SMALL_W), lambda c: (0, 0)),
                  pl.BlockSpec((8, L), lambda c: (0, 0)),
                  cast_in],
        out_specs=[pl.BlockSpec((bsz, L, VW), lambda c: (0, c, 0)), cast_out],
        out_shape=[jax.ShapeDtypeStruct((bsz, t, VW), BF16), cast_shape],
        scratch_shapes=[pltpu.VMEM((bsz, L + CONV_HALO, 2 * QK), F32),
                        pltpu.VMEM((bsz, HEADS, DK, DV), F32),
                        pltpu.VMEM((bsz, 8, DK), F32),
                        pltpu.VMEM((bsz, 8, 128), F32)],
        compiler_params=_cparams(("arbitrary",)),
        name="mlstm",
    )(proj, proj, proj, small, small_t, conv_w, bias_col, bias_row, cast_src)


def _half_block_ref(x, w):
    n_rows, n = x.shape
    if 2 * w >= 8:
        x3 = x.reshape(n_rows // (2 * w), 2 * w, n)
        return jnp.broadcast_to(x3[:, w - 1:w, :], x3.shape).reshape(n_rows, n)
    x3 = x.reshape(n_rows // 8, 8, n)
    sub = lax.broadcasted_iota(jnp.int32, x3.shape, 1)
    groups = 8 // (2 * w)
    pick = lambda g: jnp.broadcast_to(x3[:, g * 2 * w + w - 1:g * 2 * w + w, :], x3.shape)
    out = pick(groups - 1)
    for g in reversed(range(groups - 1)):
        out = jnp.where(sub < (g + 1) * 2 * w, pick(g), out)
    return out.reshape(n_rows, n)


def _gla_sequence(qk_ref, v_ref, og_ref, gc_ref, wg_ref, bg_ref, gn_ref, o_ref, s_st):
    L = GLA_CHUNK

    @pl.when(pl.program_id(0) == 0)
    def _():
        s_st[...] = jnp.zeros_like(s_st)

    z = _dot(gc_ref[...].astype(BF16), wg_ref[...]) + bg_ref[...]
    log_a = _log_sigmoid(z) * (1.0 / GATE_TAU)
    cum = _tri_left(_tri(L, True), log_a)

    row = lax.broadcasted_iota(jnp.int32, (L, L), 0)
    col = lax.broadcasted_iota(jnp.int32, (L, L), 1)
    level = jnp.where(row > col, 31 - lax.clz(row ^ col), -1)
    diag = row == col
    eye = (lax.broadcasted_iota(jnp.int32, (DK, DK), 0)
           == lax.broadcasted_iota(jnp.int32, (DK, DK), 1))
    n_levels = L.bit_length() - 1

    for h in range(HEADS):
        q_f = qk_ref[:, h * DK:(h + 1) * DK].astype(F32) * (DK ** -0.5)
        k_f = qk_ref[:, QK + h * DK:QK + (h + 1) * DK].astype(F32)
        v = v_ref[:, h * DV:(h + 1) * DV].astype(BF16)
        cum_h = cum[:, h * DK:(h + 1) * DK]

        att = jnp.where(diag, jnp.sum(q_f * k_f, axis=1, keepdims=True), 0.0)
        for lw in range(n_levels):
            f = jnp.exp(-jnp.abs(cum_h - _half_block_ref(cum_h, 1 << lw)))
            p = _dot_nt((q_f * f).astype(BF16), (k_f * f).astype(BF16))
            att = att + jnp.where(level == lw, p, 0.0)

        st = s_st[h]
        o = _dot(att.astype(BF16), v) + _dot((q_f * jnp.exp(cum_h)).astype(BF16), st.astype(BF16))
        last = cum_h[L - 1:L, :]
        k_dec = (k_f * jnp.exp(last - cum_h)).astype(BF16)
        kv = _dot_tn(k_dec, v)
        decay_col = jnp.sum(jnp.where(eye, jnp.exp(last), 0.0), axis=1, keepdims=True)
        s_st[h] = decay_col * st + kv

        o = o * lax.rsqrt(jnp.mean(o * o, axis=1, keepdims=True) + EPS)
        og = og_ref[:, h * DV:(h + 1) * DV].astype(F32)
        y = o * gn_ref[:, h * DV:(h + 1) * DV] * (og * _sigmoid(og))
        o_ref[:, h * DV:(h + 1) * DV] = y.astype(o_ref.dtype)


def _gla_kernel(qk_ref, v_ref, og_ref, gc_ref, wg_ref, bg_ref, gn_ref, wsrc_ref,
                o_ref, wdst_ref, s_st):
    wdst_ref[...] = wsrc_ref[...].astype(wdst_ref.dtype)
    for b in range(qk_ref.shape[0]):
        _gla_sequence(qk_ref.at[b], v_ref.at[b], og_ref.at[b], gc_ref.at[b], wg_ref, bg_ref,
                      gn_ref, o_ref.at[b], s_st.at[b])


def gla_mixer(proj, small, wg_pad, bg, gn, qk_col, v_col, og_col, cast_src, layer):
    bsz, t, _ = proj.shape
    L = GLA_CHUNK
    assert t % L == 0
    qk_blk, v_blk, og_blk = qk_col // (2 * QK), v_col // VW, og_col // VW
    cast_in, cast_out, cast_shape = _ride_along_specs(cast_src, layer, t // L)
    return pl.pallas_call(
        _gla_kernel,
        grid=(t // L,),
        in_specs=[pl.BlockSpec((bsz, L, 2 * QK), lambda c: (0, c, qk_blk)),
                  pl.BlockSpec((bsz, L, VW), lambda c: (0, c, v_blk)),
                  pl.BlockSpec((bsz, L, VW), lambda c: (0, c, og_blk)),
                  pl.BlockSpec((bsz, L, SMALL_W), lambda c: (0, c, 0)),
                  pl.BlockSpec((SMALL_W, QK), lambda c: (0, 0)),
                  pl.BlockSpec((1, QK), lambda c: (0, 0)),
                  pl.BlockSpec((1, VW), lambda c: (0, 0)),
                  cast_in],
        out_specs=[pl.BlockSpec((bsz, L, VW), lambda c: (0, c, 0)), cast_out],
        out_shape=[jax.ShapeDtypeStruct((bsz, t, VW), BF16), cast_shape],
        scratch_shapes=[pltpu.VMEM((bsz, HEADS, DK, DV), F32)],
        compiler_params=_cparams(("arbitrary",)),
        name="gla",
    )(proj, proj, proj, small, wg_pad, bg, gn, cast_src)


def _projection_layout(d_model):
    a_w = 2 * QK + 2 * VW
    src_a = 0
    src_if = a_w
    src_b = src_if + 2 * HEADS
    src_z = src_b + a_w
    src_g = src_z + GATE_RANK
    end = src_g + 2 * d_model
    main = [(src_a, src_if), (src_b, src_z), (src_g, end)]
    small = [(src_if, src_b), (src_z, src_g)]
    cols = dict(a_qk=0, a_v=2 * QK, a_og=2 * QK + VW, b_qk=a_w, b_v=a_w + 2 * QK,
                b_og=a_w + 2 * QK + VW, ga=2 * a_w, gb=2 * a_w + d_model)
    return main, small, cols, end


def _layer(h, hg, ssq, bsz, t_pad, l, p, w, cols, tiles, next_gain):
    m, d = h.shape
    proj = normed_matmul(hg, ssq, w['in_main_t'], l, BF16, tiles['tm'], tiles['tn_in'],
                         w_is_nk=True, relu2=False, name="in_proj")
    small_p = normed_matmul(hg, ssq, w['in_small_t'], l, F32, tiles['tm'], SMALL_W,
                            w_is_nk=True, relu2=False, name="in_proj_small")

    proj3 = proj.reshape(bsz, t_pad, proj.shape[1])
    small3 = small_p.reshape(bsz, t_pad, SMALL_W)
    small_t = jnp.swapaxes(small3[:, :, :2 * HEADS], 1, 2)
    bias_col = jnp.pad(p['b_if'], (0, SMALL_W - 2 * HEADS)).reshape(1, SMALL_W)
    bias_row = jnp.broadcast_to(p['b_if'].reshape(2 * HEADS, 1), (2 * HEADS, MLSTM_CHUNK))
    ya, w_down = mlstm_mixer(proj3, small3, small_t, p['conv_qk'], bias_col, bias_row,
                             cols['a_qk'], cols['a_v'], cols['a_og'], w['down_f32'], l)

    wg_pad = jnp.zeros((SMALL_W, QK), F32).at[2 * HEADS:2 * HEADS + GATE_RANK].set(p['w_gla_gate'])
    yb, w_up = gla_mixer(proj3, small3, wg_pad.astype(BF16), p['b_gla_gate'].reshape(1, QK),
                         p['norm_gla'].reshape(1, VW), cols['b_qk'], cols['b_v'], cols['b_og'],
                         w['up_f32'], l)

    merged = merge(ya.reshape(m, VW), yb.reshape(m, VW), w['br_a'], w['br_b'], l, proj,
                   cols['ga'], cols['gb'], tiles['tm'], tiles['tn_merge'])
    h, hg, ssq = residual_matmul(merged, w['out'], l, h, p['norm_mlp'], tiles['tm'],
                                 tiles['tn_out'], name="out_proj")
    u = normed_matmul(hg, ssq, w_up[None], 0, BF16, tiles['tm'], tiles['tn_up'],
                      w_is_nk=False, relu2=True, name="up_proj")
    return down_proj(u, w_down[None], 0, h, next_gain, tiles['tm'], tiles['tn_down'], tiles['tk'])


def kernel(x, meta, norm_mix, w_in, conv_qk, b_if, w_gla_gate, b_gla_gate, norm_gla, w_br_a,
           w_br_b, w_out, norm_mlp, w_up, w_down, norm_final):
    bsz, seq, d = x.shape
    depth = w_in.shape[0]
    n = N_META + seq
    t_pad = -(-n // T_ALIGN) * T_ALIGN
    h = jnp.concatenate([jnp.broadcast_to(meta.astype(x.dtype)[None], (bsz, N_META, d)), x,
                         jnp.zeros((bsz, t_pad - n, d), x.dtype)], axis=1)
    h = h.reshape(bsz * t_pad, d)

    main, small, cols, p_in = _projection_layout(d)
    assert w_in.shape[2] == p_in
    w_in_t = jnp.swapaxes(w_in, 1, 2)
    w = dict(in_main_t=pack_rows(w_in_t, main, tile=512),
             in_small_t=gather_rows(w_in_t, small, SMALL_W), br_a=w_br_a.astype(BF16),
             br_b=w_br_b.astype(BF16), out=w_out.astype(BF16), up_f32=w_up, down_f32=w_down)

    tiles = dict(tm=1280, tn_in=1024, tn_merge=1024, tn_out=512, tn_up=1024, tn_down=1024, tk=2048)
    hg, ssq = gain_stats(h, norm_mix[0])
    for l in range(depth):
        p = dict(conv_qk=conv_qk[l], b_if=b_if[l], w_gla_gate=w_gla_gate[l],
                 b_gla_gate=b_gla_gate[l], norm_gla=norm_gla[l], norm_mlp=norm_mlp[l])
        next_gain = norm_mix[l + 1] if l + 1 < depth else norm_final
        h, hg, ssq = _layer(h, hg, ssq, bsz, t_pad, l, p, w, cols, tiles, next_gain)
    return final_norm(h, norm_final, bsz, N_META, seq)
```

```python
import functools

import jax
import jax.numpy as jnp
from jax import lax
from jax.experimental import pallas as pl
from jax.experimental.pallas import tpu as pltpu

F32 = jnp.float32
BF16 = jnp.bfloat16

N_META = 16
EPS = 1e-6
HEADS = 4
DK = 128
DV = 256
QK = HEADS * DK
VW = HEADS * DV
CONV_W = 4
GATE_RANK = 16
GATE_TAU = 16.0
LANES = 128
SMALL_W = LANES
CONV_HALO = 8

MLSTM_CHUNK = 128
GLA_CHUNK = 128
T_ALIGN = 128

VMEM_LIMIT_BYTES = 60 * 1024 * 1024


def _cparams(sem):
    return pltpu.CompilerParams(dimension_semantics=sem, vmem_limit_bytes=VMEM_LIMIT_BYTES)


def _tile(n, pref):
    if n <= pref:
        return n
    t = (pref // LANES) * LANES
    while t >= LANES:
        if n % t == 0:
            return t
        t -= LANES
    raise ValueError(f"no 128-aligned tile for {n}")


def _sigmoid(x):
    return 0.5 * jnp.tanh(0.5 * x) + 0.5


def _log_sigmoid(x):
    return jnp.minimum(x, 0.0) - jnp.log1p(jnp.exp(-jnp.abs(x)))


def _split3(x):
    hi = x.astype(BF16)
    r = x - hi.astype(F32)
    mid = r.astype(BF16)
    lo = (r - mid.astype(F32)).astype(BF16)
    return hi, mid, lo


def _dot(a, b):
    return jnp.dot(a, b, preferred_element_type=F32)


def _dot_nt(a, b):
    return lax.dot_general(a, b, (((1,), (1,)), ((), ())), preferred_element_type=F32)


def _dot_tn(a, b):
    return lax.dot_general(a, b, (((0,), (0,)), ((), ())), preferred_element_type=F32)


def _tri_left(tri, x):
    hi, mid, lo = _split3(x)
    return (_dot(tri, lo) + _dot(tri, mid)) + _dot(tri, hi)


def _tri_right(x, tri):
    hi, mid, lo = _split3(x)
    return (_dot(lo, tri) + _dot(mid, tri)) + _dot(hi, tri)


def _tri(n, lower):
    r = lax.broadcasted_iota(jnp.int32, (n, n), 0)
    c = lax.broadcasted_iota(jnp.int32, (n, n), 1)
    keep = (r >= c) if lower else (r <= c)
    return jnp.where(keep, 1.0, 0.0).astype(BF16)


def _fold_lanes(x):
    out = x[:, 0:LANES]
    for c in range(1, x.shape[1] // LANES):
        out = out + x[:, c * LANES:(c + 1) * LANES]
    return out


def _row_scale(ssq, d):
    return lax.rsqrt(jnp.sum(ssq, axis=1, keepdims=True) * (1.0 / d) + EPS)


def _rmsnorm_kernel(x_ref, g_ref, o_ref):
    x = x_ref[...]
    ms = jnp.mean(x * x, axis=-1, keepdims=True)
    o_ref[...] = (x * lax.rsqrt(ms + EPS) * g_ref[...]).astype(o_ref.dtype)


def _gain_stats_kernel(x_ref, g_ref, hg_ref, sq_ref):
    x = x_ref[...]
    hg_ref[...] = (x * g_ref[...]).astype(hg_ref.dtype)
    sq_ref[...] = _fold_lanes(x * x)


def gain_stats(x, g, rows=256):
    m, d = x.shape
    tr = _tile(m, rows)
    return pl.pallas_call(
        _gain_stats_kernel,
        grid=(m // tr,),
        in_specs=[pl.BlockSpec((tr, d), lambda i: (i, 0)),
                  pl.BlockSpec((1, d), lambda i: (0, 0))],
        out_specs=[pl.BlockSpec((tr, d), lambda i: (i, 0)),
                   pl.BlockSpec((tr, LANES), lambda i: (i, 0))],
        out_shape=[jax.ShapeDtypeStruct((m, d), BF16),
                   jax.ShapeDtypeStruct((m, LANES), F32)],
        compiler_params=_cparams(("parallel",)),
        name="gain_stats",
    )(x, g.reshape(1, d))


def final_norm(h, g, bsz, first_row, n_rows, rows=256):
    m, d = h.shape
    t = m // bsz
    tr = _tile(n_rows, rows)
    assert first_row % 8 == 0 and t % 8 == 0
    window = pl.BlockSpec((pl.Element(tr), pl.Element(d)),
                          lambda b, i: (pl.multiple_of(b * t + first_row + i * tr, 8), 0))
    return pl.pallas_call(
        _rmsnorm_kernel,
        grid=(bsz, n_rows // tr),
        in_specs=[window, pl.BlockSpec((1, d), lambda b, i: (0, 0))],
        out_specs=pl.BlockSpec((None, tr, d), lambda b, i: (b, i, 0)),
        out_shape=jax.ShapeDtypeStruct((bsz, n_rows, d), h.dtype),
        compiler_params=_cparams(("parallel", "parallel")),
        name="final_norm",
    )(h, g.reshape(1, d))


PACK_HALO = 32


def _pack_rows_kernel(a_ref, b_ref, o_ref, *, shifts):
    rows = o_ref.shape[0]
    j = pl.program_id(1)
    for idx, (first, shift) in enumerate(shifts):
        cond = j >= first
        if idx + 1 < len(shifts):
            cond = jnp.logical_and(cond, j < shifts[idx + 1][0])

        @pl.when(cond)
        def _(shift=shift):
            if shift == 0:
                o_ref[...] = a_ref[...].astype(o_ref.dtype)
            else:
                wide = jnp.concatenate([a_ref[shift:rows, :], b_ref[0:shift, :]], axis=0)
                o_ref[...] = wide.astype(o_ref.dtype)


def pack_rows(w, segments, tile):
    layers, r, k = w.shape
    shifts, start = [], 0
    for a, b in segments:
        assert start % tile == 0 and (a - start) % 8 == 0 and 0 <= a - start <= PACK_HALO
        shifts.append((start // tile, a - start))
        start += b - a
    assert start % tile == 0 and tile % PACK_HALO == 0
    return pl.pallas_call(
        functools.partial(_pack_rows_kernel, shifts=tuple(shifts)),
        grid=(layers, start // tile),
        in_specs=[pl.BlockSpec((None, tile, k), lambda l, j: (l, j, 0)),
                  pl.BlockSpec((None, PACK_HALO, k), lambda l, j: (l, (j + 1) * (tile // PACK_HALO), 0))],
        out_specs=pl.BlockSpec((None, tile, k), lambda l, j: (l, j, 0)),
        out_shape=jax.ShapeDtypeStruct((layers, start, k), BF16),
        compiler_params=_cparams(("parallel", "parallel")),
        name="pack_rows",
    )(w, w)


def _gather_rows_kernel(*refs):
    *in_refs, o_ref = refs
    pad = o_ref.shape[0] - 8 * len(in_refs)
    parts = [r[...] for r in in_refs] + [jnp.zeros((pad, o_ref.shape[1]), F32)]
    o_ref[...] = jnp.concatenate(parts, axis=0).astype(o_ref.dtype)


def gather_rows(w, segments, n_out):
    layers, _, k = w.shape
    starts = []
    for a, b in segments:
        assert a % 8 == 0 and (b - a) % 8 == 0
        starts += list(range(a // 8, b // 8))
    return pl.pallas_call(
        _gather_rows_kernel,
        grid=(layers,),
        in_specs=[pl.BlockSpec((None, 8, k), functools.partial(lambda l, s: (l, s, 0), s=s))
                  for s in starts],
        out_specs=pl.BlockSpec((None, n_out, k), lambda l: (l, 0, 0)),
        out_shape=jax.ShapeDtypeStruct((layers, n_out, k), BF16),
        compiler_params=_cparams(("parallel",)),
        name="gather_rows",
    )(*([w] * len(starts)))


def _normed_kernel(x_ref, w_ref, ssq_ref, o_ref, r_ref, *, w_is_nk, relu2):
    @pl.when(pl.program_id(1) == 0)
    def _():
        r_ref[...] = _row_scale(ssq_ref[...], x_ref.shape[1])

    acc = _dot_nt(x_ref[...], w_ref[...]) if w_is_nk else _dot(x_ref[...], w_ref[...])
    acc = acc * r_ref[...]
    if relu2:
        acc = jnp.maximum(acc, 0.0)
        acc = acc * acc
    o_ref[...] = acc.astype(o_ref.dtype)


def normed_matmul(hg, ssq, w, layer, out_dtype, tm, tn, *, w_is_nk, relu2, name):
    m, k = hg.shape
    n = w.shape[1] if w_is_nk else w.shape[2]
    tm, tn = _tile(m, tm), _tile(n, tn)
    if w_is_nk:
        w_spec = pl.BlockSpec((None, tn, k), lambda i, j: (layer, j, 0))
    else:
        w_spec = pl.BlockSpec((None, k, tn), lambda i, j: (layer, 0, j))
    return pl.pallas_call(
        functools.partial(_normed_kernel, w_is_nk=w_is_nk, relu2=relu2),
        grid=(m // tm, n // tn),
        in_specs=[pl.BlockSpec((tm, k), lambda i, j: (i, 0)),
                  w_spec,
                  pl.BlockSpec((tm, ssq.shape[1]), lambda i, j: (i, 0))],
        out_specs=pl.BlockSpec((tm, tn), lambda i, j: (i, j)),
        out_shape=jax.ShapeDtypeStruct((m, n), out_dtype),
        scratch_shapes=[pltpu.VMEM((tm, 1), F32)],
        compiler_params=_cparams(("parallel", "arbitrary")),
        name=name,
    )(hg, w, ssq)


def _emit_stats(h_new, g_ref, hg_ref, sq_ref, first):
    hg_ref[...] = (h_new * g_ref[...]).astype(hg_ref.dtype)
    part = _fold_lanes(h_new * h_new)

    @pl.when(first)
    def _():
        sq_ref[...] = part

    @pl.when(jnp.logical_not(first))
    def _():
        sq_ref[...] += part


def _residual_kernel(x_ref, w_ref, h_ref, g_ref, o_ref, hg_ref, sq_ref):
    h_new = h_ref[...] + _dot(x_ref[...], w_ref[...])
    o_ref[...] = h_new
    _emit_stats(h_new, g_ref, hg_ref, sq_ref, pl.program_id(1) == 0)


def _stats_out(m, n, tm, tn, tile_index, row_index):
    specs = [pl.BlockSpec((tm, tn), tile_index), pl.BlockSpec((tm, tn), tile_index),
             pl.BlockSpec((tm, LANES), row_index)]
    shapes = [jax.ShapeDtypeStruct((m, n), F32), jax.ShapeDtypeStruct((m, n), BF16),
              jax.ShapeDtypeStruct((m, LANES), F32)]
    return specs, shapes


def residual_matmul(x, w, layer, h, gain, tm, tn, name):
    m, k = x.shape
    n = w.shape[2]
    tm, tn = _tile(m, tm), _tile(n, tn)
    out_specs, out_shape = _stats_out(m, n, tm, tn, lambda i, j: (i, j), lambda i, j: (i, 0))
    return pl.pallas_call(
        _residual_kernel,
        grid=(m // tm, n // tn),
        in_specs=[pl.BlockSpec((tm, k), lambda i, j: (i, 0)),
                  pl.BlockSpec((None, k, tn), lambda i, j: (layer, 0, j)),
                  pl.BlockSpec((tm, tn), lambda i, j: (i, j)),
                  pl.BlockSpec((1, tn), lambda i, j: (0, j))],
        out_specs=out_specs,
        out_shape=out_shape,
        compiler_params=_cparams(("parallel", "arbitrary")),
        name=name,
    )(x, w, h, gain.reshape(1, n))


def _down_kernel(u_ref, w_ref, h_ref, g_ref, o_ref, hg_ref, sq_ref, acc_ref):
    kk = pl.program_id(2)
    last = pl.num_programs(2) - 1
    first_tile = pl.program_id(1) == 0

    @pl.when(kk == 0)
    def _():
        acc_ref[...] = h_ref[...] + _dot(u_ref[...], w_ref[...])

    @pl.when(jnp.logical_and(kk > 0, kk < last))
    def _():
        acc_ref[...] += _dot(u_ref[...], w_ref[...])

    @pl.when(kk == last)
    def _():
        h_new = acc_ref[...] + _dot(u_ref[...], w_ref[...])
        o_ref[...] = h_new
        _emit_stats(h_new, g_ref, hg_ref, sq_ref, first_tile)


def down_proj(u, w, layer, h, gain, tm, tn, tk):
    m, k = u.shape
    n = w.shape[2]
    tm, tn, tk = _tile(m, tm), _tile(n, tn), _tile(k, tk)
    assert k // tk >= 2
    out_specs, out_shape = _stats_out(m, n, tm, tn, lambda i, j, kk: (i, j),
                                      lambda i, j, kk: (i, 0))
    return pl.pallas_call(
        _down_kernel,
        grid=(m // tm, n // tn, k // tk),
        in_specs=[pl.BlockSpec((tm, tk), lambda i, j, kk: (i, kk)),
                  pl.BlockSpec((None, tk, tn), lambda i, j, kk: (layer, kk, j)),
                  pl.BlockSpec((tm, tn), lambda i, j, kk: (i, j)),
                  pl.BlockSpec((1, tn), lambda i, j, kk: (0, j))],
        out_specs=out_specs,
        out_shape=out_shape,
        scratch_shapes=[pltpu.VMEM((tm, tn), F32)],
        compiler_params=_cparams(("parallel", "arbitrary", "arbitrary")),
        name="down_proj",
    )(u, w, h, gain.reshape(1, n))


def _merge_kernel(ya_ref, yb_ref, wa_ref, wb_ref, ga_ref, gb_ref, o_ref):
    a = _dot(ya_ref[...], wa_ref[...])
    b = _dot(yb_ref[...], wb_ref[...])
    ga = _sigmoid(ga_ref[...].astype(F32))
    gb = _sigmoid(gb_ref[...].astype(F32))
    o_ref[...] = (ga * a + gb * b).astype(o_ref.dtype)


def merge(ya, yb, wa, wb, layer, proj, ga_col, gb_col, tm, tn):
    m, k = ya.shape
    n = wa.shape[2]
    tm, tn = _tile(m, tm), _tile(n, tn)
    ga_blk, gb_blk = ga_col // tn, gb_col // tn
    assert ga_blk * tn == ga_col and gb_blk * tn == gb_col
    return pl.pallas_call(
        _merge_kernel,
        grid=(m // tm, n // tn),
        in_specs=[pl.BlockSpec((tm, k), lambda i, j: (i, 0)),
                  pl.BlockSpec((tm, k), lambda i, j: (i, 0)),
                  pl.BlockSpec((None, k, tn), lambda i, j: (layer, 0, j)),
                  pl.BlockSpec((None, k, tn), lambda i, j: (layer, 0, j)),
                  pl.BlockSpec((tm, tn), lambda i, j: (i, ga_blk + j)),
                  pl.BlockSpec((tm, tn), lambda i, j: (i, gb_blk + j))],
        out_specs=pl.BlockSpec((tm, tn), lambda i, j: (i, j)),
        out_shape=jax.ShapeDtypeStruct((m, n), BF16),
        compiler_params=_cparams(("parallel", "parallel")),
        name="merge",
    )(ya, yb, wa, wb, proj, proj)


def _mlstm_sequence(qk_ref, v_ref, og_ref, gc_ref, gr_ref, convw_ref, bc_ref, br_ref,
                    o_ref, xbuf, c_st, n_st, m_st):
    L = MLSTM_CHUNK

    @pl.when(pl.program_id(0) == 0)
    def _():
        xbuf[0:CONV_HALO, :] = jnp.zeros((CONV_HALO, 2 * QK), F32)
        c_st[...] = jnp.zeros_like(c_st)
        n_st[...] = jnp.zeros_like(n_st)
        m_st[...] = jnp.zeros_like(m_st)

    xbuf[CONV_HALO:CONV_HALO + L, :] = qk_ref[...].astype(F32)
    w = convw_ref[...]
    acc = w[CONV_W - 1:CONV_W, :] * xbuf[CONV_HALO:CONV_HALO + L, :]
    for j in range(CONV_W - 1):
        back = CONV_W - 1 - j
        acc = acc + w[j:j + 1, :] * xbuf[CONV_HALO - back:CONV_HALO - back + L, :]
    xbuf[0:CONV_HALO, :] = xbuf[L:L + CONV_HALO, :]
    qk = acc * _sigmoid(acc)

    g_col = gc_ref[...] + bc_ref[...]
    g_row = gr_ref[...] + br_ref[...]
    b_col = _tri_left(_tri(L, True), _log_sigmoid(g_col))
    b_row = _tri_right(_log_sigmoid(g_row), _tri(L, False))

    row = lax.broadcasted_iota(jnp.int32, (L, L), 0)
    col = lax.broadcasted_iota(jnp.int32, (L, L), 1)
    causal = row >= col

    for h in range(HEADS):
        q_f = qk[:, h * DK:(h + 1) * DK]
        q = q_f.astype(BF16)
        k_f = qk[:, QK + h * DK:QK + (h + 1) * DK] * (DK ** -0.5)
        k = k_f.astype(BF16)
        v = v_ref[:, h * DV:(h + 1) * DV].astype(BF16)
        v_f = v.astype(F32)
        li_c = g_col[:, h:h + 1]
        li_r = g_row[h:h + 1, :]
        b_c = b_col[:, HEADS + h:HEADS + h + 1]
        b_r = b_row[HEADS + h:HEADS + h + 1, :]
        m_prev = m_st[h:h + 1, 0:1]
        ct = c_st[h]
        n_row = n_st[h:h + 1, :]

        log_d = jnp.where(causal, b_c - b_r + li_r, -jnp.inf)
        m_t = jnp.maximum(b_c + m_prev, jnp.max(log_d, axis=1, keepdims=True))
        s = _dot_nt(q, k) * jnp.exp(log_d - m_t)
        carry_w = jnp.exp(b_c + m_prev - m_t)
        num = _dot(s.astype(BF16), v) + carry_w * _dot(q, ct.astype(BF16))
        qn = jnp.sum(q_f * n_row, axis=1, keepdims=True)
        den = jnp.sum(s, axis=1, keepdims=True) + carry_w * qn
        hh = num / jnp.maximum(jnp.abs(den), jnp.exp(-m_t))
        og = og_ref[:, h * DV:(h + 1) * DV].astype(F32)
        o_ref[:, h * DV:(h + 1) * DV] = (hh * _sigmoid(og)).astype(o_ref.dtype)

        g_tot = b_c[L - 1:L, :]
        log_w = g_tot - b_c + li_c
        m_new = jnp.maximum(g_tot + m_prev, jnp.max(log_w, axis=0, keepdims=True))
        wgt = jnp.exp(log_w - m_new)
        decay = jnp.exp(g_tot + m_prev - m_new)
        kv = _dot_tn(k, (wgt * v_f).astype(BF16))
        c_st[h] = decay * ct + kv
        n_st[h:h + 1, :] = decay * n_row + jnp.sum(wgt * k_f, axis=0, keepdims=True)
        m_st[h:h + 1, :] = jnp.broadcast_to(m_new, (1, m_st.shape[1]))


def _ride_along_specs(src, layer, steps):
    _, r, c = src.shape
    rb = next(rb for rb in range(16, r + 1, 16) if r % rb == 0 and r // rb <= steps)
    last = r // rb - 1
    in_spec = pl.BlockSpec((None, rb, c), lambda s: (layer, jnp.minimum(s, last), 0))
    out_spec = pl.BlockSpec((rb, c), lambda s: (jnp.minimum(s, last), 0))
    return in_spec, out_spec, jax.ShapeDtypeStruct((r, c), BF16)


def _mlstm_kernel(qk_ref, v_ref, og_ref, gc_ref, gr_ref, convw_ref, bc_ref, br_ref, wsrc_ref,
                  o_ref, wdst_ref, xbuf, c_st, n_st, m_st):
    wdst_ref[...] = wsrc_ref[...].astype(wdst_ref.dtype)
    for b in range(qk_ref.shape[0]):
        _mlstm_sequence(qk_ref.at[b], v_ref.at[b], og_ref.at[b], gc_ref.at[b], gr_ref.at[b],
                        convw_ref, bc_ref, br_ref, o_ref.at[b], xbuf.at[b], c_st.at[b],
                        n_st.at[b], m_st.at[b])


def mlstm_mixer(proj, small, small_t, conv_w, bias_col, bias_row, qk_col, v_col, og_col,
                cast_src, layer):
    bsz, t, _ = proj.shape
    L = MLSTM_CHUNK
    assert t % L == 0
    qk_blk, v_blk, og_blk = qk_col // (2 * QK), v_col // VW, og_col // VW
    cast_in, cast_out, cast_shape = _ride_along_specs(cast_src, layer, t // L)
    return pl.pallas_call(
        _mlstm_kernel,
        grid=(t // L,),
        in_specs=[pl.BlockSpec((bsz, L, 2 * QK), lambda c: (0, c, qk_blk)),
                  pl.BlockSpec((bsz, L, VW), lambda c: (0, c, v_blk)),
                  pl.BlockSpec((bsz, L, VW), lambda c: (0, c, og_blk)),
                  pl.BlockSpec((bsz, L, SMALL_W), lambda c: (0, c, 0)),
                  pl.BlockSpec((bsz, 8, L), lambda c: (0, 0, c)),
                  pl.BlockSpec((CONV_W, 2 * QK), lambda c: (0, 0)),
                  pl.BlockSpec((1, SMALL_W), lambda c: (0, 0)),
                  pl.BlockSpec((8, L), lambda c: (0, 0)),
                  cast_in],
        out_specs=[pl.BlockSpec((bsz, L, VW), lambda c: (0, c, 0)), cast_out],
        out_shape=[jax.ShapeDtypeStruct((bsz, t, VW), BF16), cast_shape],
        scratch_shapes=[pltpu.VMEM((bsz, L + CONV_HALO, 2 * QK), F32),
                        pltpu.VMEM((bsz, HEADS, DK, DV), F32),
                        pltpu.VMEM((bsz, 8, DK), F32),
                        pltpu.VMEM((bsz, 8, 128), F32)],
        compiler_params=_cparams(("arbitrary",)),
        name="mlstm",
    )(proj, proj, proj, small, small_t, conv_w, bias_col, bias_row, cast_src)


def _half_block_ref(x, w):
    n_rows, n = x.shape
    if 2 * w >= 8:
        x3 = x.reshape(n_rows // (2 * w), 2 * w, n)
        return jnp.broadcast_to(x3[:, w - 1:w, :], x3.shape).reshape(n_rows, n)
    x3 = x.reshape(n_rows // 8, 8, n)
    sub = lax.broadcasted_iota(jnp.int32, x3.shape, 1)
    groups = 8 // (2 * w)
    pick = lambda g: jnp.broadcast_to(x3[:, g * 2 * w + w - 1:g * 2 * w + w, :], x3.shape)
    out = pick(groups - 1)
    for g in reversed(range(groups - 1)):
        out = jnp.where(sub < (g + 1) * 2 * w, pick(g), out)
    return out.reshape(n_rows, n)


def _gla_kernel(qk_ref, v_ref, og_ref, gc_ref, wg_ref, bg_ref, gn_ref, wsrc_ref,
                o_ref, wdst_ref, s_st):
    L = GLA_CHUNK
    seqs = range(qk_ref.shape[0])
    heads = range(HEADS)
    n_levels = L.bit_length() - 1
    lanes = lambda h, w: slice(h * w, (h + 1) * w)

    wdst_ref[...] = wsrc_ref[...].astype(wdst_ref.dtype)

    @pl.when(pl.program_id(0) == 0)
    def _():
        s_st[...] = jnp.zeros_like(s_st)

    row = lax.broadcasted_iota(jnp.int32, (L, L), 0)
    col = lax.broadcasted_iota(jnp.int32, (L, L), 1)
    level = jnp.where(row > col, 31 - lax.clz(row ^ col), -1)
    diag = row == col
    eye = (lax.broadcasted_iota(jnp.int32, (DK, DK), 0)
           == lax.broadcasted_iota(jnp.int32, (DK, DK), 1))
    tril = _tri(L, True)

    cum, q_f, k_f = [], [], []
    for b in seqs:
        z = _dot(gc_ref[b].astype(BF16), wg_ref[...]) + bg_ref[...]
        cum.append(_tri_left(tril, _log_sigmoid(z) * (1.0 / GATE_TAU)))
        q_f.append(qk_ref[b, :, 0:QK].astype(F32) * (DK ** -0.5))
        k_f.append(qk_ref[b, :, QK:2 * QK].astype(F32))

    p = {}
    for b in seqs:
        for lw in range(n_levels):
            f = jnp.exp(-jnp.abs(cum[b] - _half_block_ref(cum[b], 1 << lw)))
            qf = (q_f[b] * f).astype(BF16)
            kf = (k_f[b] * f).astype(BF16)
            for h in heads:
                p[b, h, lw] = _dot_nt(qf[:, lanes(h, DK)], kf[:, lanes(h, DK)])

    q_hat, k_dec, last = [], [], []
    for b in seqs:
        last.append(cum[b][L - 1:L, :])
        q_hat.append((q_f[b] * jnp.exp(cum[b])).astype(BF16))
        k_dec.append((k_f[b] * jnp.exp(last[b] - cum[b])).astype(BF16))

    for b in seqs:
        qk_prod = q_f[b] * k_f[b]
        for h in heads:
            att = jnp.where(diag, jnp.sum(qk_prod[:, lanes(h, DK)], axis=1, keepdims=True), 0.0)
            for lw in range(n_levels):
                att = att + jnp.where(level == lw, p[b, h, lw], 0.0)

            v = v_ref[b, :, lanes(h, DV)].astype(BF16)
            st = s_st[b, h]
            o = _dot(att.astype(BF16), v) + _dot(q_hat[b][:, lanes(h, DK)], st.astype(BF16))
            kv = _dot_tn(k_dec[b][:, lanes(h, DK)], v)
            decay_col = jnp.sum(jnp.where(eye, jnp.exp(last[b][:, lanes(h, DK)]), 0.0),
                                axis=1, keepdims=True)
            s_st[b, h] = decay_col * st + kv

            o = o * lax.rsqrt(jnp.mean(o * o, axis=1, keepdims=True) + EPS)
            og = og_ref[b, :, lanes(h, DV)].astype(F32)
            y = o * gn_ref[:, lanes(h, DV)] * (og * _sigmoid(og))
            o_ref[b, :, lanes(h, DV)] = y.astype(o_ref.dtype)


def gla_mixer(proj, small, wg_pad, bg, gn, qk_col, v_col, og_col, cast_src, layer):
    bsz, t, _ = proj.shape
    L = GLA_CHUNK
    assert t % L == 0
    qk_blk, v_blk, og_blk = qk_col // (2 * QK), v_col // VW, og_col // VW
    cast_in, cast_out, cast_shape = _ride_along_specs(cast_src, layer, t // L)
    return pl.pallas_call(
        _gla_kernel,
        grid=(t // L,),
        in_specs=[pl.BlockSpec((bsz, L, 2 * QK), lambda c: (0, c, qk_blk)),
                  pl.BlockSpec((bsz, L, VW), lambda c: (0, c, v_blk)),
                  pl.BlockSpec((bsz, L, VW), lambda c: (0, c, og_blk)),
                  pl.BlockSpec((bsz, L, SMALL_W), lambda c: (0, c, 0)),
                  pl.BlockSpec((SMALL_W, QK), lambda c: (0, 0)),
                  pl.BlockSpec((1, QK), lambda c: (0, 0)),
                  pl.BlockSpec((1, VW), lambda c: (0, 0)),
                  cast_in],
        out_specs=[pl.BlockSpec((bsz, L, VW), lambda c: (0, c, 0)), cast_out],
        out_shape=[jax.ShapeDtypeStruct((bsz, t, VW), BF16), cast_shape],
        scratch_shapes=[pltpu.VMEM((bsz, HEADS, DK, DV), F32)],
        compiler_params=_cparams(("arbitrary",)),
        name="gla",
    )(proj, proj, proj, small, wg_pad, bg, gn, cast_src)


def _projection_layout(d_model):
    a_w = 2 * QK + 2 * VW
    src_a = 0
    src_if = a_w
    src_b = src_if + 2 * HEADS
    src_z = src_b + a_w
    src_g = src_z + GATE_RANK
    end = src_g + 2 * d_model
    main = [(src_a, src_if), (src_b, src_z), (src_g, end)]
    small = [(src_if, src_b), (src_z, src_g)]
    cols = dict(a_qk=0, a_v=2 * QK, a_og=2 * QK + VW, b_qk=a_w, b_v=a_w + 2 * QK,
                b_og=a_w + 2 * QK + VW, ga=2 * a_w, gb=2 * a_w + d_model)
    return main, small, cols, end


def _layer(h, hg, ssq, bsz, t_pad, l, p, w, cols, tiles, next_gain):
    m, d = h.shape
    proj = normed_matmul(hg, ssq, w['in_main_t'], l, BF16, tiles['tm'], tiles['tn_in'],
                         w_is_nk=True, relu2=False, name="in_proj")
    small_p = normed_matmul(hg, ssq, w['in_small_t'], l, F32, tiles['tm'], SMALL_W,
                            w_is_nk=True, relu2=False, name="in_proj_small")

    proj3 = proj.reshape(bsz, t_pad, proj.shape[1])
    small3 = small_p.reshape(bsz, t_pad, SMALL_W)
    small_t = jnp.swapaxes(small3[:, :, :2 * HEADS], 1, 2)
    bias_col = jnp.pad(p['b_if'], (0, SMALL_W - 2 * HEADS)).reshape(1, SMALL_W)
    bias_row = jnp.broadcast_to(p['b_if'].reshape(2 * HEADS, 1), (2 * HEADS, MLSTM_CHUNK))
    ya, w_down = mlstm_mixer(proj3, small3, small_t, p['conv_qk'], bias_col, bias_row,
                             cols['a_qk'], cols['a_v'], cols['a_og'], w['down_f32'], l)

    wg_pad = jnp.zeros((SMALL_W, QK), F32).at[2 * HEADS:2 * HEADS + GATE_RANK].set(p['w_gla_gate'])
    yb, w_up = gla_mixer(proj3, small3, wg_pad.astype(BF16), p['b_gla_gate'].reshape(1, QK),
                         p['norm_gla'].reshape(1, VW), cols['b_qk'], cols['b_v'], cols['b_og'],
                         w['up_f32'], l)

    merged = merge(ya.reshape(m, VW), yb.reshape(m, VW), w['br_a'], w['br_b'], l, proj,
                   cols['ga'], cols['gb'], tiles['tm'], tiles['tn_merge'])
    h, hg, ssq = residual_matmul(merged, w['out'], l, h, p['norm_mlp'], tiles['tm'],
                                 tiles['tn_out'], name="out_proj")
    u = normed_matmul(hg, ssq, w_up[None], 0, BF16, tiles['tm'], tiles['tn_up'],
                      w_is_nk=False, relu2=True, name="up_proj")
    return down_proj(u, w_down[None], 0, h, next_gain, tiles['tm'], tiles['tn_down'], tiles['tk'])


def kernel(x, meta, norm_mix, w_in, conv_qk, b_if, w_gla_gate, b_gla_gate, norm_gla, w_br_a,
           w_br_b, w_out, norm_mlp, w_up, w_down, norm_final):
    bsz, seq, d = x.shape
    depth = w_in.shape[0]
    n = N_META + seq
    t_pad = -(-n // T_ALIGN) * T_ALIGN
    h = jnp.concatenate([jnp.broadcast_to(meta.astype(x.dtype)[None], (bsz, N_META, d)), x,
                         jnp.zeros((bsz, t_pad - n, d), x.dtype)], axis=1)
    h = h.reshape(bsz * t_pad, d)

    main, small, cols, p_in = _projection_layout(d)
    assert w_in.shape[2] == p_in
    w_in_t = jnp.swapaxes(w_in, 1, 2)
    w = dict(in_main_t=pack_rows(w_in_t, main, tile=512),
             in_small_t=gather_rows(w_in_t, small, SMALL_W), br_a=w_br_a.astype(BF16),
             br_b=w_br_b.astype(BF16), out=w_out.astype(BF16), up_f32=w_up, down_f32=w_down)

    tiles = dict(tm=1280, tn_in=1024, tn_merge=1024, tn_out=512, tn_up=1024, tn_down=1024, tk=2048)
    hg, ssq = gain_stats(h, norm_mix[0])
    for l in range(depth):
        p = dict(conv_qk=conv_qk[l], b_if=b_if[l], w_gla_gate=w_gla_gate[l],
                 b_gla_gate=b_gla_gate[l], norm_gla=norm_gla[l], norm_mlp=norm_mlp[l])
        next_gain = norm_mix[l + 1] if l + 1 < depth else norm_final
        h, hg, ssq = _layer(h, hg, ssq, bsz, t_pad, l, p, w, cols, tiles, next_gain)
    return final_norm(h, norm_final, bsz, N_META, seq)
```

```python
import functools

import jax
import jax.numpy as jnp
from jax import lax
from jax.experimental import pallas as pl
from jax.experimental.pallas import tpu as pltpu

F32 = jnp.float32
BF16 = jnp.bfloat16

N_META = 16
EPS = 1e-6
HEADS = 4
DK = 128
DV = 256
QK = HEADS * DK
VW = HEADS * DV
CONV_W = 4
GATE_RANK = 16
GATE_TAU = 16.0
LANES = 128
SMALL_W = LANES
CONV_HALO = 8

MLSTM_CHUNK = 128
GLA_CHUNK = 128
T_ALIGN = 128

VMEM_LIMIT_BYTES = 62 * 1024 * 1024


def _cparams(sem):
    return pltpu.CompilerParams(dimension_semantics=sem, vmem_limit_bytes=VMEM_LIMIT_BYTES)


def _tile(n, pref):
    if n <= pref:
        return n
    t = (pref // LANES) * LANES
    while t >= LANES:
        if n % t == 0:
            return t
        t -= LANES
    raise ValueError(f"no 128-aligned tile for {n}")


def _sigmoid(x):
    return 0.5 * jnp.tanh(0.5 * x) + 0.5


def _log_sigmoid(x):
    return jnp.minimum(x, 0.0) - jnp.log1p(jnp.exp(-jnp.abs(x)))


def _split3(x):
    hi = x.astype(BF16)
    r = x - hi.astype(F32)
    mid = r.astype(BF16)
    lo = (r - mid.astype(F32)).astype(BF16)
    return hi, mid, lo


def _dot(a, b):
    return jnp.dot(a, b, preferred_element_type=F32)


def _dot_nt(a, b):
    return lax.dot_general(a, b, (((1,), (1,)), ((), ())), preferred_element_type=F32)


def _dot_tn(a, b):
    return lax.dot_general(a, b, (((0,), (0,)), ((), ())), preferred_element_type=F32)


def _tri_left(tri, x):
    hi, mid, lo = _split3(x)
    return (_dot(tri, lo) + _dot(tri, mid)) + _dot(tri, hi)


def _tri_right(x, tri):
    hi, mid, lo = _split3(x)
    return (_dot(lo, tri) + _dot(mid, tri)) + _dot(hi, tri)


def _tri(n, lower):
    r = lax.broadcasted_iota(jnp.int32, (n, n), 0)
    c = lax.broadcasted_iota(jnp.int32, (n, n), 1)
    keep = (r >= c) if lower else (r <= c)
    return jnp.where(keep, 1.0, 0.0).astype(BF16)


def _fold_lanes(x):
    out = x[:, 0:LANES]
    for c in range(1, x.shape[1] // LANES):
        out = out + x[:, c * LANES:(c + 1) * LANES]
    return out


def _row_scale(ssq, d):
    return lax.rsqrt(jnp.sum(ssq, axis=1, keepdims=True) * (1.0 / d) + EPS)


def _rmsnorm_kernel(x_ref, g_ref, o_ref):
    x = x_ref[...]
    ms = jnp.mean(x * x, axis=-1, keepdims=True)
    o_ref[...] = (x * lax.rsqrt(ms + EPS) * g_ref[...]).astype(o_ref.dtype)


def _gain_stats_kernel(x_ref, g_ref, hg_ref, sq_ref):
    x = x_ref[...]
    hg_ref[...] = (x * g_ref[...]).astype(hg_ref.dtype)
    sq_ref[...] = _fold_lanes(x * x)


def gain_stats(x, g, rows=256):
    m, d = x.shape
    tr = _tile(m, rows)
    return pl.pallas_call(
        _gain_stats_kernel,
        grid=(m // tr,),
        in_specs=[pl.BlockSpec((tr, d), lambda i: (i, 0)),
                  pl.BlockSpec((1, d), lambda i: (0, 0))],
        out_specs=[pl.BlockSpec((tr, d), lambda i: (i, 0)),
                   pl.BlockSpec((tr, LANES), lambda i: (i, 0))],
        out_shape=[jax.ShapeDtypeStruct((m, d), BF16),
                   jax.ShapeDtypeStruct((m, LANES), F32)],
        compiler_params=_cparams(("parallel",)),
        name="gain_stats",
    )(x, g.reshape(1, d))


def final_norm(h, g, bsz, first_row, n_rows, rows=256):
    m, d = h.shape
    t = m // bsz
    tr = _tile(n_rows, rows)
    assert first_row % 8 == 0 and t % 8 == 0
    window = pl.BlockSpec((pl.Element(tr), pl.Element(d)),
                          lambda b, i: (pl.multiple_of(b * t + first_row + i * tr, 8), 0))
    return pl.pallas_call(
        _rmsnorm_kernel,
        grid=(bsz, n_rows // tr),
        in_specs=[window, pl.BlockSpec((1, d), lambda b, i: (0, 0))],
        out_specs=pl.BlockSpec((None, tr, d), lambda b, i: (b, i, 0)),
        out_shape=jax.ShapeDtypeStruct((bsz, n_rows, d), h.dtype),
        compiler_params=_cparams(("parallel", "parallel")),
        name="final_norm",
    )(h, g.reshape(1, d))


def _packed_window(w, layer, segments, tile, tile_index):
    layers, r, k = w.shape
    first_tile, shift, start = [], [], 0
    for a, b in segments:
        assert start % tile == 0 and a % 8 == 0 and r % 8 == 0
        first_tile.append(start // tile)
        shift.append(a - start)
        start += b - a
    assert start % tile == 0

    def src_row(*grid_ids):
        t = tile_index(*grid_ids)
        s = shift[0]
        for f, sh in zip(first_tile[1:], shift[1:]):
            s = jnp.where(t >= f, sh, s)
        return pl.multiple_of(layer * r + t * tile + s, 8), 0

    spec = pl.BlockSpec((pl.Element(tile), pl.Element(k)), src_row)
    return w.reshape(layers * r, k), spec, start


def _cast_kernel(x_ref, o_ref):
    o_ref[...] = x_ref[...].astype(o_ref.dtype)


def pack_rows(w, layer, segments, tile):
    w2, window, n_rows = _packed_window(w, layer, segments, tile, lambda j: j)
    return pl.pallas_call(
        _cast_kernel,
        grid=(n_rows // tile,),
        in_specs=[window],
        out_specs=pl.BlockSpec((tile, w2.shape[1]), lambda j: (j, 0)),
        out_shape=jax.ShapeDtypeStruct((n_rows, w2.shape[1]), BF16),
        compiler_params=_cparams(("parallel",)),
        name="pack_rows",
    )(w2)


def _gather_rows_kernel(*refs):
    *in_refs, o_ref = refs
    pad = o_ref.shape[0] - 8 * len(in_refs)
    parts = [r[...] for r in in_refs] + [jnp.zeros((pad, o_ref.shape[1]), F32)]
    o_ref[...] = jnp.concatenate(parts, axis=0).astype(o_ref.dtype)


def gather_rows(w, segments, n_out):
    layers, _, k = w.shape
    starts = []
    for a, b in segments:
        assert a % 8 == 0 and (b - a) % 8 == 0
        starts += list(range(a // 8, b // 8))
    return pl.pallas_call(
        _gather_rows_kernel,
        grid=(layers,),
        in_specs=[pl.BlockSpec((None, 8, k), functools.partial(lambda l, s: (l, s, 0), s=s))
                  for s in starts],
        out_specs=pl.BlockSpec((None, n_out, k), lambda l: (l, 0, 0)),
        out_shape=jax.ShapeDtypeStruct((layers, n_out, k), BF16),
        compiler_params=_cparams(("parallel",)),
        name="gather_rows",
    )(*([w] * len(starts)))


def _normed_kernel(x_ref, w_ref, ssq_ref, o_ref, r_ref, *, w_is_nk, relu2):
    @pl.when(pl.program_id(1) == 0)
    def _():
        r_ref[...] = _row_scale(ssq_ref[...], x_ref.shape[1])

    acc = _dot_nt(x_ref[...], w_ref[...]) if w_is_nk else _dot(x_ref[...], w_ref[...])
    acc = acc * r_ref[...]
    if relu2:
        acc = jnp.maximum(acc, 0.0)
        acc = acc * acc
    o_ref[...] = acc.astype(o_ref.dtype)


def normed_matmul(hg, ssq, w, layer, out_dtype, tm, tn, *, w_is_nk, relu2, name):
    m, k = hg.shape
    n = w.shape[1] if w_is_nk else w.shape[2]
    tm, tn = _tile(m, tm), _tile(n, tn)
    if w_is_nk:
        w_spec = pl.BlockSpec((None, tn, k), lambda i, j: (layer, j, 0))
    else:
        w_spec = pl.BlockSpec((None, k, tn), lambda i, j: (layer, 0, j))
    return pl.pallas_call(
        functools.partial(_normed_kernel, w_is_nk=w_is_nk, relu2=relu2),
        grid=(m // tm, n // tn),
        in_specs=[pl.BlockSpec((tm, k), lambda i, j: (i, 0)),
                  w_spec,
                  pl.BlockSpec((tm, ssq.shape[1]), lambda i, j: (i, 0))],
        out_specs=pl.BlockSpec((tm, tn), lambda i, j: (i, j)),
        out_shape=jax.ShapeDtypeStruct((m, n), out_dtype),
        scratch_shapes=[pltpu.VMEM((tm, 1), F32)],
        compiler_params=_cparams(("parallel", "arbitrary")),
        name=name,
    )(hg, w, ssq)


def _in_proj_pack_kernel(x_ref, w_ref, ssq_ref, wsrc_ref, o_ref, packed_ref, r_ref):
    packed_ref[...] = wsrc_ref[...].astype(packed_ref.dtype)
    _normed_kernel(x_ref, w_ref, ssq_ref, o_ref, r_ref, w_is_nk=True, relu2=False)


def in_proj_pack(hg, ssq, w_packed, w_next, next_layer, segments, tm, tn):
    m, k = hg.shape
    n = w_packed.shape[0]
    tm, tn = _tile(m, tm), _tile(n, tn)
    n_j = n // tn
    pack_tile = next(t for t in (128, 256, 512, 1024) if n // t <= (m // tm) * n_j)
    n_tiles = n // pack_tile
    tile_index = lambda i, j: jnp.minimum(i * n_j + j, n_tiles - 1)
    w2, window, n_rows = _packed_window(w_next, next_layer, segments, pack_tile, tile_index)
    assert n_rows == n
    return pl.pallas_call(
        _in_proj_pack_kernel,
        grid=(m // tm, n_j),
        in_specs=[pl.BlockSpec((tm, k), lambda i, j: (i, 0)),
                  pl.BlockSpec((tn, k), lambda i, j: (j, 0)),
                  pl.BlockSpec((tm, ssq.shape[1]), lambda i, j: (i, 0)),
                  window],
        out_specs=[pl.BlockSpec((tm, tn), lambda i, j: (i, j)),
                   pl.BlockSpec((pack_tile, k), lambda i, j: (tile_index(i, j), 0))],
        out_shape=[jax.ShapeDtypeStruct((m, n), BF16),
                   jax.ShapeDtypeStruct((n_rows, k), BF16)],
        scratch_shapes=[pltpu.VMEM((tm, 1), F32)],
        compiler_params=_cparams(("arbitrary", "arbitrary")),
        name="in_proj",
    )(hg, w_packed, ssq, w2)


def _emit_stats(h_new, g_ref, hg_ref, sq_ref, first):
    hg_ref[...] = (h_new * g_ref[...]).astype(hg_ref.dtype)
    part = _fold_lanes(h_new * h_new)

    @pl.when(first)
    def _():
        sq_ref[...] = part

    @pl.when(jnp.logical_not(first))
    def _():
        sq_ref[...] += part


def _residual_kernel(x_ref, w_ref, h_ref, g_ref, o_ref, hg_ref, sq_ref):
    h_new = h_ref[...] + _dot(x_ref[...], w_ref[...])
    o_ref[...] = h_new
    _emit_stats(h_new, g_ref, hg_ref, sq_ref, pl.program_id(1) == 0)


def _stats_out(m, n, tm, tn, tile_index, row_index):
    specs = [pl.BlockSpec((tm, tn), tile_index), pl.BlockSpec((tm, tn), tile_index),
             pl.BlockSpec((tm, LANES), row_index)]
    shapes = [jax.ShapeDtypeStruct((m, n), F32), jax.ShapeDtypeStruct((m, n), BF16),
              jax.ShapeDtypeStruct((m, LANES), F32)]
    return specs, shapes


def residual_matmul(x, w, layer, h, gain, tm, tn, name):
    m, k = x.shape
    n = w.shape[2]
    tm, tn = _tile(m, tm), _tile(n, tn)
    out_specs, out_shape = _stats_out(m, n, tm, tn, lambda i, j: (i, j), lambda i, j: (i, 0))
    return pl.pallas_call(
        _residual_kernel,
        grid=(m // tm, n // tn),
        in_specs=[pl.BlockSpec((tm, k), lambda i, j: (i, 0)),
                  pl.BlockSpec((None, k, tn), lambda i, j: (layer, 0, j)),
                  pl.BlockSpec((tm, tn), lambda i, j: (i, j)),
                  pl.BlockSpec((1, tn), lambda i, j: (0, j))],
        out_specs=out_specs,
        out_shape=out_shape,
        compiler_params=_cparams(("parallel", "arbitrary")),
        name=name,
    )(x, w, h, gain.reshape(1, n))


def _down_kernel(u_ref, w_ref, h_ref, g_ref, o_ref, hg_ref, sq_ref, acc_ref):
    kk = pl.program_id(2)
    last = pl.num_programs(2) - 1
    first_tile = pl.program_id(1) == 0

    @pl.when(kk == 0)
    def _():
        acc_ref[...] = h_ref[...] + _dot(u_ref[...], w_ref[...])

    @pl.when(jnp.logical_and(kk > 0, kk < last))
    def _():
        acc_ref[...] += _dot(u_ref[...], w_ref[...])

    @pl.when(kk == last)
    def _():
        h_new = acc_ref[...] + _dot(u_ref[...], w_ref[...])
        o_ref[...] = h_new
        _emit_stats(h_new, g_ref, hg_ref, sq_ref, first_tile)


def down_proj(u, w, layer, h, gain, tm, tn, tk):
    m, k = u.shape
    n = w.shape[2]
    tm, tn, tk = _tile(m, tm), _tile(n, tn), _tile(k, tk)
    assert k // tk >= 2
    out_specs, out_shape = _stats_out(m, n, tm, tn, lambda i, j, kk: (i, j),
                                      lambda i, j, kk: (i, 0))
    return pl.pallas_call(
        _down_kernel,
        grid=(m // tm, n // tn, k // tk),
        in_specs=[pl.BlockSpec((tm, tk), lambda i, j, kk: (i, kk)),
                  pl.BlockSpec((None, tk, tn), lambda i, j, kk: (layer, kk, j)),
                  pl.BlockSpec((tm, tn), lambda i, j, kk: (i, j)),
                  pl.BlockSpec((1, tn), lambda i, j, kk: (0, j))],
        out_specs=out_specs,
        out_shape=out_shape,
        scratch_shapes=[pltpu.VMEM((tm, tn), F32)],
        compiler_params=_cparams(("parallel", "arbitrary", "arbitrary")),
        name="down_proj",
    )(u, w, h, gain.reshape(1, n))


def _merge_kernel(ya_ref, yb_ref, wa_ref, wb_ref, ga_ref, gb_ref, o_ref):
    a = _dot(ya_ref[...], wa_ref[...])
    b = _dot(yb_ref[...], wb_ref[...])
    ga = _sigmoid(ga_ref[...].astype(F32))
    gb = _sigmoid(gb_ref[...].astype(F32))
    o_ref[...] = (ga * a + gb * b).astype(o_ref.dtype)


def merge(ya, yb, wa, wb, layer, proj, ga_col, gb_col, tm, tn):
    m, k = ya.shape
    n = wa.shape[2]
    tm, tn = _tile(m, tm), _tile(n, tn)
    ga_blk, gb_blk = ga_col // tn, gb_col // tn
    assert ga_blk * tn == ga_col and gb_blk * tn == gb_col
    return pl.pallas_call(
        _merge_kernel,
        grid=(m // tm, n // tn),
        in_specs=[pl.BlockSpec((tm, k), lambda i, j: (i, 0)),
                  pl.BlockSpec((tm, k), lambda i, j: (i, 0)),
                  pl.BlockSpec((None, k, tn), lambda i, j: (layer, 0, j)),
                  pl.BlockSpec((None, k, tn), lambda i, j: (layer, 0, j)),
                  pl.BlockSpec((tm, tn), lambda i, j: (i, ga_blk + j)),
                  pl.BlockSpec((tm, tn), lambda i, j: (i, gb_blk + j))],
        out_specs=pl.BlockSpec((tm, tn), lambda i, j: (i, j)),
        out_shape=jax.ShapeDtypeStruct((m, n), BF16),
        compiler_params=_cparams(("parallel", "parallel")),
        name="merge",
    )(ya, yb, wa, wb, proj, proj)


def _mlstm_sequence(qk_ref, v_ref, og_ref, gc_ref, gr_ref, convw_ref, bc_ref, br_ref,
                    o_ref, xbuf, c_st, n_st, m_st):
    L = MLSTM_CHUNK

    @pl.when(pl.program_id(0) == 0)
    def _():
        xbuf[0:CONV_HALO, :] = jnp.zeros((CONV_HALO, 2 * QK), F32)
        c_st[...] = jnp.zeros_like(c_st)
        n_st[...] = jnp.zeros_like(n_st)
        m_st[...] = jnp.zeros_like(m_st)

    xbuf[CONV_HALO:CONV_HALO + L, :] = qk_ref[...].astype(F32)
    w = convw_ref[...]
    acc = w[CONV_W - 1:CONV_W, :] * xbuf[CONV_HALO:CONV_HALO + L, :]
    for j in range(CONV_W - 1):
        back = CONV_W - 1 - j
        acc = acc + w[j:j + 1, :] * xbuf[CONV_HALO - back:CONV_HALO - back + L, :]
    xbuf[0:CONV_HALO, :] = xbuf[L:L + CONV_HALO, :]
    qk = acc * _sigmoid(acc)

    g_col = gc_ref[...] + bc_ref[...]
    g_row = gr_ref[...] + br_ref[...]
    b_col = _tri_left(_tri(L, True), _log_sigmoid(g_col))
    b_row = _tri_right(_log_sigmoid(g_row), _tri(L, False))

    row = lax.broadcasted_iota(jnp.int32, (L, L), 0)
    col = lax.broadcasted_iota(jnp.int32, (L, L), 1)
    causal = row >= col

    for h in range(HEADS):
        q_f = qk[:, h * DK:(h + 1) * DK]
        q = q_f.astype(BF16)
        k_f = qk[:, QK + h * DK:QK + (h + 1) * DK] * (DK ** -0.5)
        k = k_f.astype(BF16)
        v = v_ref[:, h * DV:(h + 1) * DV].astype(BF16)
        v_f = v.astype(F32)
        li_c = g_col[:, h:h + 1]
        li_r = g_row[h:h + 1, :]
        b_c = b_col[:, HEADS + h:HEADS + h + 1]
        b_r = b_row[HEADS + h:HEADS + h + 1, :]
        m_prev = m_st[h:h + 1, 0:1]
        ct = c_st[h]
        n_row = n_st[h:h + 1, :]

        log_d = jnp.where(causal, b_c - b_r + li_r, -jnp.inf)
        m_t = jnp.maximum(b_c + m_prev, jnp.max(log_d, axis=1, keepdims=True))
        s = _dot_nt(q, k) * jnp.exp(log_d - m_t)
        carry_w = jnp.exp(b_c + m_prev - m_t)
        num = _dot(s.astype(BF16), v) + carry_w * _dot(q, ct.astype(BF16))
        qn = jnp.sum(q_f * n_row, axis=1, keepdims=True)
        den = jnp.sum(s, axis=1, keepdims=True) + carry_w * qn
        hh = num / jnp.maximum(jnp.abs(den), jnp.exp(-m_t))
        og = og_ref[:, h * DV:(h + 1) * DV].astype(F32)
        o_ref[:, h * DV:(h + 1) * DV] = (hh * _sigmoid(og)).astype(o_ref.dtype)

        g_tot = b_c[L - 1:L, :]
        log_w = g_tot - b_c + li_c
        m_new = jnp.maximum(g_tot + m_prev, jnp.max(log_w, axis=0, keepdims=True))
        wgt = jnp.exp(log_w - m_new)
        decay = jnp.exp(g_tot + m_prev - m_new)
        kv = _dot_tn(k, (wgt * v_f).astype(BF16))
        c_st[h] = decay * ct + kv
        n_st[h:h + 1, :] = decay * n_row + jnp.sum(wgt * k_f, axis=0, keepdims=True)
        m_st[h:h + 1, :] = jnp.broadcast_to(m_new, (1, m_st.shape[1]))


def _ride_along_specs(src, layer, steps):
    _, r, c = src.shape
    rb = next(rb for rb in range(16, r + 1, 16) if r % rb == 0 and r // rb <= steps)
    last = r // rb - 1
    in_spec = pl.BlockSpec((None, rb, c), lambda s: (layer, jnp.minimum(s, last), 0))
    out_spec = pl.BlockSpec((rb, c), lambda s: (jnp.minimum(s, last), 0))
    return in_spec, out_spec, jax.ShapeDtypeStruct((r, c), BF16)


def _mlstm_kernel(qk_ref, v_ref, og_ref, gc_ref, gr_ref, convw_ref, bc_ref, br_ref, wsrc_ref,
                  o_ref, wdst_ref, xbuf, c_st, n_st, m_st):
    wdst_ref[...] = wsrc_ref[...].astype(wdst_ref.dtype)
    for b in range(qk_ref.shape[0]):
        _mlstm_sequence(qk_ref.at[b], v_ref.at[b], og_ref.at[b], gc_ref.at[b], gr_ref.at[b],
                        convw_ref, bc_ref, br_ref, o_ref.at[b], xbuf.at[b], c_st.at[b],
                        n_st.at[b], m_st.at[b])


def mlstm_mixer(proj, small, small_t, conv_w, bias_col, bias_row, qk_col, v_col, og_col,
                cast_src, layer):
    bsz, t, _ = proj.shape
    L = MLSTM_CHUNK
    assert t % L == 0
    qk_blk, v_blk, og_blk = qk_col // (2 * QK), v_col // VW, og_col // VW
    cast_in, cast_out, cast_shape = _ride_along_specs(cast_src, layer, t // L)
    return pl.pallas_call(
        _mlstm_kernel,
        grid=(t // L,),
        in_specs=[pl.BlockSpec((bsz, L, 2 * QK), lambda c: (0, c, qk_blk)),
                  pl.BlockSpec((bsz, L, VW), lambda c: (0, c, v_blk)),
                  pl.BlockSpec((bsz, L, VW), lambda c: (0, c, og_blk)),
                  pl.BlockSpec((bsz, L, SMALL_W), lambda c: (0, c, 0)),
                  pl.BlockSpec((bsz, 8, L), lambda c: (0, 0, c)),
                  pl.BlockSpec((CONV_W, 2 * QK), lambda c: (0, 0)),
                  pl.BlockSpec((1, SMALL_W), lambda c: (0, 0)),
                  pl.BlockSpec((8, L), lambda c: (0, 0)),
                  cast_in],
        out_specs=[pl.BlockSpec((bsz, L, VW), lambda c: (0, c, 0)), cast_out],
        out_shape=[jax.ShapeDtypeStruct((bsz, t, VW), BF16), cast_shape],
        scratch_shapes=[pltpu.VMEM((bsz, L + CONV_HALO, 2 * QK), F32),
                        pltpu.VMEM((bsz, HEADS, DK, DV), F32),
                        pltpu.VMEM((bsz, 8, DK), F32),
                        pltpu.VMEM((bsz, 8, 128), F32)],
        compiler_params=_cparams(("arbitrary",)),
        name="mlstm",
    )(proj, proj, proj, small, small_t, conv_w, bias_col, bias_row, cast_src)


def _half_block_ref(x, w):
    n_rows, n = x.shape
    if 2 * w >= 8:
        x3 = x.reshape(n_rows // (2 * w), 2 * w, n)
        return jnp.broadcast_to(x3[:, w - 1:w, :], x3.shape).reshape(n_rows, n)
    x3 = x.reshape(n_rows // 8, 8, n)
    sub = lax.broadcasted_iota(jnp.int32, x3.shape, 1)
    groups = 8 // (2 * w)
    pick = lambda g: jnp.broadcast_to(x3[:, g * 2 * w + w - 1:g * 2 * w + w, :], x3.shape)
    out = pick(groups - 1)
    for g in reversed(range(groups - 1)):
        out = jnp.where(sub < (g + 1) * 2 * w, pick(g), out)
    return out.reshape(n_rows, n)


def _gla_kernel(qk_ref, v_ref, og_ref, gc_ref, wg_ref, bg_ref, gn_ref, wsrc_ref,
                o_ref, wdst_ref, s_st):
    L = GLA_CHUNK
    seqs = range(qk_ref.shape[0])
    heads = range(HEADS)
    n_levels = L.bit_length() - 1
    lanes = lambda h, w: slice(h * w, (h + 1) * w)

    wdst_ref[...] = wsrc_ref[...].astype(wdst_ref.dtype)

    @pl.when(pl.program_id(0) == 0)
    def _():
        s_st[...] = jnp.zeros_like(s_st)

    row = lax.broadcasted_iota(jnp.int32, (L, L), 0)
    col = lax.broadcasted_iota(jnp.int32, (L, L), 1)
    level = jnp.where(row > col, 31 - lax.clz(row ^ col), -1)
    diag = row == col
    eye = (lax.broadcasted_iota(jnp.int32, (DK, DK), 0)
           == lax.broadcasted_iota(jnp.int32, (DK, DK), 1))
    tril = _tri(L, True)

    cum, q_f, k_f = [], [], []
    for b in seqs:
        z = _dot(gc_ref[b].astype(BF16), wg_ref[...]) + bg_ref[...]
        cum.append(_tri_left(tril, _log_sigmoid(z) * (1.0 / GATE_TAU)))
        q_f.append(qk_ref[b, :, 0:QK].astype(F32) * (DK ** -0.5))
        k_f.append(qk_ref[b, :, QK:2 * QK].astype(F32))

    p = {}
    for b in seqs:
        for lw in range(n_levels):
            f = jnp.exp(-jnp.abs(cum[b] - _half_block_ref(cum[b], 1 << lw)))
            qf = (q_f[b] * f).astype(BF16)
            kf = (k_f[b] * f).astype(BF16)
            for h in heads:
                p[b, h, lw] = _dot_nt(qf[:, lanes(h, DK)], kf[:, lanes(h, DK)])

    q_hat, k_dec, last = [], [], []
    for b in seqs:
        last.append(cum[b][L - 1:L, :])
        q_hat.append((q_f[b] * jnp.exp(cum[b])).astype(BF16))
        k_dec.append((k_f[b] * jnp.exp(last[b] - cum[b])).astype(BF16))

    for b in seqs:
        qk_prod = q_f[b] * k_f[b]
        for h in heads:
            att = jnp.where(diag, jnp.sum(qk_prod[:, lanes(h, DK)], axis=1, keepdims=True), 0.0)
            for lw in range(n_levels):
                att = att + jnp.where(level == lw, p[b, h, lw], 0.0)

            v = v_ref[b, :, lanes(h, DV)].astype(BF16)
            st = s_st[b, h]
            o = _dot(att.astype(BF16), v) + _dot(q_hat[b][:, lanes(h, DK)], st.astype(BF16))
            kv = _dot_tn(k_dec[b][:, lanes(h, DK)], v)
            decay_col = jnp.sum(jnp.where(eye, jnp.exp(last[b][:, lanes(h, DK)]), 0.0),
                                axis=1, keepdims=True)
            s_st[b, h] = decay_col * st + kv

            o = o * lax.rsqrt(jnp.mean(o * o, axis=1, keepdims=True) + EPS)
            og = og_ref[b, :, lanes(h, DV)].astype(F32)
            y = o * gn_ref[:, lanes(h, DV)] * (og * _sigmoid(og))
            o_ref[b, :, lanes(h, DV)] = y.astype(o_ref.dtype)


def gla_mixer(proj, small, wg_pad, bg, gn, qk_col, v_col, og_col, cast_src, layer):
    bsz, t, _ = proj.shape
    L = GLA_CHUNK
    assert t % L == 0
    qk_blk, v_blk, og_blk = qk_col // (2 * QK), v_col // VW, og_col // VW
    cast_in, cast_out, cast_shape = _ride_along_specs(cast_src, layer, t // L)
    return pl.pallas_call(
        _gla_kernel,
        grid=(t // L,),
        in_specs=[pl.BlockSpec((bsz, L, 2 * QK), lambda c: (0, c, qk_blk)),
                  pl.BlockSpec((bsz, L, VW), lambda c: (0, c, v_blk)),
                  pl.BlockSpec((bsz, L, VW), lambda c: (0, c, og_blk)),
                  pl.BlockSpec((bsz, L, SMALL_W), lambda c: (0, c, 0)),
                  pl.BlockSpec((SMALL_W, QK), lambda c: (0, 0)),
                  pl.BlockSpec((1, QK), lambda c: (0, 0)),
                  pl.BlockSpec((1, VW), lambda c: (0, 0)),
                  cast_in],
        out_specs=[pl.BlockSpec((bsz, L, VW), lambda c: (0, c, 0)), cast_out],
        out_shape=[jax.ShapeDtypeStruct((bsz, t, VW), BF16), cast_shape],
        scratch_shapes=[pltpu.VMEM((bsz, HEADS, DK, DV), F32)],
        compiler_params=_cparams(("arbitrary",)),
        name="gla",
    )(proj, proj, proj, small, wg_pad, bg, gn, cast_src)


def _projection_layout(d_model):
    a_w = 2 * QK + 2 * VW
    src_a = 0
    src_if = a_w
    src_b = src_if + 2 * HEADS
    src_z = src_b + a_w
    src_g = src_z + GATE_RANK
    end = src_g + 2 * d_model
    main = [(src_a, src_if), (src_b, src_z), (src_g, end)]
    small = [(src_if, src_b), (src_z, src_g)]
    cols = dict(a_qk=0, a_v=2 * QK, a_og=2 * QK + VW, b_qk=a_w, b_v=a_w + 2 * QK,
                b_og=a_w + 2 * QK + VW, ga=2 * a_w, gb=2 * a_w + d_model)
    return main, small, cols, end


def _layer(h, hg, ssq, w_in_packed, bsz, t_pad, l, p, w, cols, tiles, next_gain, last):
    m, d = h.shape
    if last:
        next_packed = None
        proj = normed_matmul(hg, ssq, w_in_packed[None], 0, BF16, tiles['tm'], tiles['tn_in'],
                             w_is_nk=True, relu2=False, name="in_proj")
    else:
        proj, next_packed = in_proj_pack(hg, ssq, w_in_packed, w['in_t_f32'], l + 1,
                                         w['in_segments'], tiles['tm'], tiles['tn_in'])
    small_p = normed_matmul(hg, ssq, w['in_small_t'], l, F32, tiles['tm'], SMALL_W,
                            w_is_nk=True, relu2=False, name="in_proj_small")

    proj3 = proj.reshape(bsz, t_pad, proj.shape[1])
    small3 = small_p.reshape(bsz, t_pad, SMALL_W)
    small_t = jnp.swapaxes(small3[:, :, :2 * HEADS], 1, 2)
    bias_col = jnp.pad(p['b_if'], (0, SMALL_W - 2 * HEADS)).reshape(1, SMALL_W)
    bias_row = jnp.broadcast_to(p['b_if'].reshape(2 * HEADS, 1), (2 * HEADS, MLSTM_CHUNK))
    ya, w_down = mlstm_mixer(proj3, small3, small_t, p['conv_qk'], bias_col, bias_row,
                             cols['a_qk'], cols['a_v'], cols['a_og'], w['down_f32'], l)

    wg_pad = jnp.zeros((SMALL_W, QK), F32).at[2 * HEADS:2 * HEADS + GATE_RANK].set(p['w_gla_gate'])
    yb, w_up = gla_mixer(proj3, small3, wg_pad.astype(BF16), p['b_gla_gate'].reshape(1, QK),
                         p['norm_gla'].reshape(1, VW), cols['b_qk'], cols['b_v'], cols['b_og'],
                         w['up_f32'], l)

    merged = merge(ya.reshape(m, VW), yb.reshape(m, VW), w['br_a'], w['br_b'], l, proj,
                   cols['ga'], cols['gb'], tiles['tm'], tiles['tn_merge'])
    h, hg, ssq = residual_matmul(merged, w['out'], l, h, p['norm_mlp'], tiles['tm'],
                                 tiles['tn_out'], name="out_proj")
    u = normed_matmul(hg, ssq, w_up[None], 0, BF16, tiles['tm'], tiles['tn_up'],
                      w_is_nk=False, relu2=True, name="up_proj")
    h, hg, ssq = down_proj(u, w_down[None], 0, h, next_gain, tiles['tm'], tiles['tn_down'],
                           tiles['tk'])
    return h, hg, ssq, next_packed


def kernel(x, meta, norm_mix, w_in, conv_qk, b_if, w_gla_gate, b_gla_gate, norm_gla, w_br_a,
           w_br_b, w_out, norm_mlp, w_up, w_down, norm_final):
    bsz, seq, d = x.shape
    depth = w_in.shape[0]
    n = N_META + seq
    t_pad = -(-n // T_ALIGN) * T_ALIGN
    h = jnp.concatenate([jnp.broadcast_to(meta.astype(x.dtype)[None], (bsz, N_META, d)), x,
                         jnp.zeros((bsz, t_pad - n, d), x.dtype)], axis=1)
    h = h.reshape(bsz * t_pad, d)

    main, small, cols, p_in = _projection_layout(d)
    assert w_in.shape[2] == p_in
    w_in_t = jnp.swapaxes(w_in, 1, 2)
    w = dict(in_t_f32=w_in_t, in_segments=main,
             in_small_t=gather_rows(w_in_t, small, SMALL_W), br_a=w_br_a.astype(BF16),
             br_b=w_br_b.astype(BF16), out=w_out.astype(BF16), up_f32=w_up, down_f32=w_down)
    w_in_packed = pack_rows(w_in_t, 0, main, tile=512)

    tiles = dict(tm=1280, tn_in=1024, tn_merge=1024, tn_out=512, tn_up=1024, tn_down=1024, tk=2048)
    hg, ssq = gain_stats(h, norm_mix[0])
    for l in range(depth):
        p = dict(conv_qk=conv_qk[l], b_if=b_if[l], w_gla_gate=w_gla_gate[l],
                 b_gla_gate=b_gla_gate[l], norm_gla=norm_gla[l], norm_mlp=norm_mlp[l])
        next_gain = norm_mix[l + 1] if l + 1 < depth else norm_final
        h, hg, ssq, w_in_packed = _layer(h, hg, ssq, w_in_packed, bsz, t_pad, l, p, w, cols, tiles,
                                         next_gain, last=l + 1 == depth)
    return final_norm(h, norm_final, bsz, N_META, seq)
```

```python
import functools

import jax
import jax.numpy as jnp
from jax import lax
from jax.experimental import pallas as pl
from jax.experimental.pallas import tpu as pltpu

F32 = jnp.float32
BF16 = jnp.bfloat16

N_META = 16
EPS = 1e-6
HEADS = 4
DK = 128
DV = 256
QK = HEADS * DK
VW = HEADS * DV
CONV_W = 4
GATE_RANK = 16
GATE_TAU = 16.0
LANES = 128
SMALL_W = LANES
CONV_HALO = 8

MLSTM_CHUNK = 128
GLA_CHUNK = 128
T_ALIGN = 128

VMEM_LIMIT_BYTES = 62 * 1024 * 1024


def _cparams(sem):
    return pltpu.CompilerParams(dimension_semantics=sem, vmem_limit_bytes=VMEM_LIMIT_BYTES)


def _tile(n, pref):
    if n <= pref:
        return n
    t = (pref // LANES) * LANES
    while t >= LANES:
        if n % t == 0:
            return t
        t -= LANES
    raise ValueError(f"no 128-aligned tile for {n}")


def _sigmoid(x):
    return 0.5 * jnp.tanh(0.5 * x) + 0.5


def _log_sigmoid(x):
    return jnp.minimum(x, 0.0) - jnp.log1p(jnp.exp(-jnp.abs(x)))


def _split3(x):
    hi = x.astype(BF16)
    r = x - hi.astype(F32)
    mid = r.astype(BF16)
    lo = (r - mid.astype(F32)).astype(BF16)
    return hi, mid, lo


def _dot(a, b):
    return jnp.dot(a, b, preferred_element_type=F32)


def _dot_nt(a, b):
    return lax.dot_general(a, b, (((1,), (1,)), ((), ())), preferred_element_type=F32)


def _dot_tn(a, b):
    return lax.dot_general(a, b, (((0,), (0,)), ((), ())), preferred_element_type=F32)


def _tri_left(tri, x):
    hi, mid, lo = _split3(x)
    return (_dot(tri, lo) + _dot(tri, mid)) + _dot(tri, hi)


def _tri_right(x, tri):
    hi, mid, lo = _split3(x)
    return (_dot(lo, tri) + _dot(mid, tri)) + _dot(hi, tri)


def _tri(n, lower):
    r = lax.broadcasted_iota(jnp.int32, (n, n), 0)
    c = lax.broadcasted_iota(jnp.int32, (n, n), 1)
    keep = (r >= c) if lower else (r <= c)
    return jnp.where(keep, 1.0, 0.0).astype(BF16)


def _fold_lanes(x):
    out = x[:, 0:LANES]
    for c in range(1, x.shape[1] // LANES):
        out = out + x[:, c * LANES:(c + 1) * LANES]
    return out


def _row_scale(ssq, d):
    return lax.rsqrt(jnp.sum(ssq, axis=1, keepdims=True) * (1.0 / d) + EPS)


def _rmsnorm_kernel(x_ref, g_ref, o_ref):
    x = x_ref[...]
    ms = jnp.mean(x * x, axis=-1, keepdims=True)
    o_ref[...] = (x * lax.rsqrt(ms + EPS) * g_ref[...]).astype(o_ref.dtype)


def _embed_kernel(x_ref, meta_ref, g_ref, h_ref, hg_ref, sq_ref, *, tail_rows):
    tr, d = h_ref.shape
    n_meta = meta_ref.shape[0]
    i = pl.program_id(1)
    last = pl.num_programs(1) - 1

    def emit(h):
        h_ref[...] = h
        hg_ref[...] = (h * g_ref[...]).astype(hg_ref.dtype)
        sq_ref[...] = _fold_lanes(h * h)

    @pl.when(i == 0)
    def _():
        emit(jnp.concatenate([meta_ref[...], x_ref[0:tr - n_meta, :]], axis=0))

    @pl.when(jnp.logical_and(i > 0, i < last))
    def _():
        emit(x_ref[...])

    @pl.when(i == last)
    def _():
        emit(jnp.concatenate([x_ref[tr - tail_rows:tr, :],
                              jnp.zeros((tr - tail_rows, d), F32)], axis=0))


def embed_stats(x, meta, g, t_pad, rows=128):
    bsz, s, d = x.shape
    n_meta = meta.shape[0]
    tr = rows
    n_blk = t_pad // tr
    tail_rows = n_meta + s - (n_blk - 1) * tr
    assert t_pad % tr == 0 and n_meta % 8 == 0 and s % 8 == 0 and n_meta < tr <= s
    assert 0 < tail_rows <= tr and tail_rows % 8 == 0 and n_blk >= 3
    window = pl.BlockSpec(
        (pl.Element(tr), pl.Element(d)),
        lambda b, i: (pl.multiple_of(b * s + jnp.clip(i * tr - n_meta, 0, s - tr), 8), 0))
    out_rows = lambda b, i: (b * n_blk + i, 0)
    m = bsz * t_pad
    return pl.pallas_call(
        functools.partial(_embed_kernel, tail_rows=tail_rows),
        grid=(bsz, n_blk),
        in_specs=[window,
                  pl.BlockSpec((n_meta, d), lambda b, i: (0, 0)),
                  pl.BlockSpec((1, d), lambda b, i: (0, 0))],
        out_specs=[pl.BlockSpec((tr, d), out_rows), pl.BlockSpec((tr, d), out_rows),
                   pl.BlockSpec((tr, LANES), out_rows)],
        out_shape=[jax.ShapeDtypeStruct((m, d), F32), jax.ShapeDtypeStruct((m, d), BF16),
                   jax.ShapeDtypeStruct((m, LANES), F32)],
        compiler_params=_cparams(("parallel", "parallel")),
        name="embed_stats",
    )(x.reshape(bsz * s, d), meta, g.reshape(1, d))


def final_norm(h, g, bsz, first_row, n_rows, rows=256):
    m, d = h.shape
    t = m // bsz
    tr = _tile(n_rows, rows)
    assert first_row % 8 == 0 and t % 8 == 0
    window = pl.BlockSpec((pl.Element(tr), pl.Element(d)),
                          lambda b, i: (pl.multiple_of(b * t + first_row + i * tr, 8), 0))
    return pl.pallas_call(
        _rmsnorm_kernel,
        grid=(bsz, n_rows // tr),
        in_specs=[window, pl.BlockSpec((1, d), lambda b, i: (0, 0))],
        out_specs=pl.BlockSpec((None, tr, d), lambda b, i: (b, i, 0)),
        out_shape=jax.ShapeDtypeStruct((bsz, n_rows, d), h.dtype),
        compiler_params=_cparams(("parallel", "parallel")),
        name="final_norm",
    )(h, g.reshape(1, d))


def _packed_window(w, layer, segments, tile, tile_index):
    layers, r, k = w.shape
    first_tile, shift, start = [], [], 0
    for a, b in segments:
        assert start % tile == 0 and a % 8 == 0 and r % 8 == 0
        first_tile.append(start // tile)
        shift.append(a - start)
        start += b - a
    assert start % tile == 0

    def src_row(*grid_ids):
        t = tile_index(*grid_ids)
        s = shift[0]
        for f, sh in zip(first_tile[1:], shift[1:]):
            s = jnp.where(t >= f, sh, s)
        return pl.multiple_of(layer * r + t * tile + s, 8), 0

    spec = pl.BlockSpec((pl.Element(tile), pl.Element(k)), src_row)
    return w.reshape(layers * r, k), spec, start


def _cast_kernel(x_ref, o_ref):
    o_ref[...] = x_ref[...].astype(o_ref.dtype)


def pack_rows(w, layer, segments, tile):
    w2, window, n_rows = _packed_window(w, layer, segments, tile, lambda j: j)
    return pl.pallas_call(
        _cast_kernel,
        grid=(n_rows // tile,),
        in_specs=[window],
        out_specs=pl.BlockSpec((tile, w2.shape[1]), lambda j: (j, 0)),
        out_shape=jax.ShapeDtypeStruct((n_rows, w2.shape[1]), BF16),
        compiler_params=_cparams(("parallel",)),
        name="pack_rows",
    )(w2)


def _gather_rows_kernel(*refs):
    *in_refs, o_ref = refs
    pad = o_ref.shape[0] - 8 * len(in_refs)
    parts = [r[...] for r in in_refs] + [jnp.zeros((pad, o_ref.shape[1]), F32)]
    o_ref[...] = jnp.concatenate(parts, axis=0).astype(o_ref.dtype)


def gather_rows(w, segments, n_out):
    layers, _, k = w.shape
    starts = []
    for a, b in segments:
        assert a % 8 == 0 and (b - a) % 8 == 0
        starts += list(range(a // 8, b // 8))
    return pl.pallas_call(
        _gather_rows_kernel,
        grid=(layers,),
        in_specs=[pl.BlockSpec((None, 8, k), functools.partial(lambda l, s: (l, s, 0), s=s))
                  for s in starts],
        out_specs=pl.BlockSpec((None, n_out, k), lambda l: (l, 0, 0)),
        out_shape=jax.ShapeDtypeStruct((layers, n_out, k), BF16),
        compiler_params=_cparams(("parallel",)),
        name="gather_rows",
    )(*([w] * len(starts)))


def _normed_kernel(x_ref, w_ref, ssq_ref, o_ref, r_ref, *, w_is_nk, relu2):
    @pl.when(pl.program_id(1) == 0)
    def _():
        r_ref[...] = _row_scale(ssq_ref[...], x_ref.shape[1])

    acc = _dot_nt(x_ref[...], w_ref[...]) if w_is_nk else _dot(x_ref[...], w_ref[...])
    acc = acc * r_ref[...]
    if relu2:
        acc = jnp.maximum(acc, 0.0)
        acc = acc * acc
    o_ref[...] = acc.astype(o_ref.dtype)


def normed_matmul(hg, ssq, w, layer, out_dtype, tm, tn, *, w_is_nk, relu2, name):
    m, k = hg.shape
    n = w.shape[1] if w_is_nk else w.shape[2]
    tm, tn = _tile(m, tm), _tile(n, tn)
    if w_is_nk:
        w_spec = pl.BlockSpec((None, tn, k), lambda i, j: (layer, j, 0))
    else:
        w_spec = pl.BlockSpec((None, k, tn), lambda i, j: (layer, 0, j))
    return pl.pallas_call(
        functools.partial(_normed_kernel, w_is_nk=w_is_nk, relu2=relu2),
        grid=(m // tm, n // tn),
        in_specs=[pl.BlockSpec((tm, k), lambda i, j: (i, 0)),
                  w_spec,
                  pl.BlockSpec((tm, ssq.shape[1]), lambda i, j: (i, 0))],
        out_specs=pl.BlockSpec((tm, tn), lambda i, j: (i, j)),
        out_shape=jax.ShapeDtypeStruct((m, n), out_dtype),
        scratch_shapes=[pltpu.VMEM((tm, 1), F32)],
        compiler_params=_cparams(("parallel", "arbitrary")),
        name=name,
    )(hg, w, ssq)


def _up_proj_cast_kernel(x_ref, w_ref, ssq_ref, wsrc_ref, o_ref, wdst_ref, r_ref):
    wdst_ref[...] = wsrc_ref[...].astype(wdst_ref.dtype)
    _normed_kernel(x_ref, w_ref, ssq_ref, o_ref, r_ref, w_is_nk=False, relu2=True)


def up_proj_cast(hg, ssq, w_up, cast_src, cast_layer, tm, tn):
    m, k = hg.shape
    n = w_up.shape[1]
    tm, tn = _tile(m, tm), _tile(n, tn)
    n_j = n // tn
    cast_in, cast_out, cast_shape = _ride_along_specs(cast_src, cast_layer, (m // tm) * n_j,
                                                      lambda i, j: i * n_j + j)
    return pl.pallas_call(
        _up_proj_cast_kernel,
        grid=(m // tm, n_j),
        in_specs=[pl.BlockSpec((tm, k), lambda i, j: (i, 0)),
                  pl.BlockSpec((k, tn), lambda i, j: (0, j)),
                  pl.BlockSpec((tm, ssq.shape[1]), lambda i, j: (i, 0)),
                  cast_in],
        out_specs=[pl.BlockSpec((tm, tn), lambda i, j: (i, j)), cast_out],
        out_shape=[jax.ShapeDtypeStruct((m, n), BF16), cast_shape],
        scratch_shapes=[pltpu.VMEM((tm, 1), F32)],
        compiler_params=_cparams(("arbitrary", "arbitrary")),
        name="up_proj",
    )(hg, w_up, ssq, cast_src)


def _in_proj_pack_kernel(x_ref, w_ref, ssq_ref, wsrc_ref, o_ref, packed_ref, r_ref):
    packed_ref[...] = wsrc_ref[...].astype(packed_ref.dtype)
    _normed_kernel(x_ref, w_ref, ssq_ref, o_ref, r_ref, w_is_nk=True, relu2=False)


def in_proj_pack(hg, ssq, w_packed, w_next, next_layer, segments, tm, tn):
    m, k = hg.shape
    n = w_packed.shape[0]
    tm, tn = _tile(m, tm), _tile(n, tn)
    n_j = n // tn
    pack_tile = next(t for t in (128, 256, 512, 1024) if n // t <= (m // tm) * n_j)
    n_tiles = n // pack_tile
    tile_index = lambda i, j: jnp.minimum(i * n_j + j, n_tiles - 1)
    w2, window, n_rows = _packed_window(w_next, next_layer, segments, pack_tile, tile_index)
    assert n_rows == n
    return pl.pallas_call(
        _in_proj_pack_kernel,
        grid=(m // tm, n_j),
        in_specs=[pl.BlockSpec((tm, k), lambda i, j: (i, 0)),
                  pl.BlockSpec((tn, k), lambda i, j: (j, 0)),
                  pl.BlockSpec((tm, ssq.shape[1]), lambda i, j: (i, 0)),
                  window],
        out_specs=[pl.BlockSpec((tm, tn), lambda i, j: (i, j)),
                   pl.BlockSpec((pack_tile, k), lambda i, j: (tile_index(i, j), 0))],
        out_shape=[jax.ShapeDtypeStruct((m, n), BF16),
                   jax.ShapeDtypeStruct((n_rows, k), BF16)],
        scratch_shapes=[pltpu.VMEM((tm, 1), F32)],
        compiler_params=_cparams(("arbitrary", "arbitrary")),
        name="in_proj",
    )(hg, w_packed, ssq, w2)


def _emit_stats(h_new, g_ref, hg_ref, sq_ref, first):
    hg_ref[...] = (h_new * g_ref[...]).astype(hg_ref.dtype)
    part = _fold_lanes(h_new * h_new)

    @pl.when(first)
    def _():
        sq_ref[...] = part

    @pl.when(jnp.logical_not(first))
    def _():
        sq_ref[...] += part


def _residual_kernel(x_hbm, w_ref, h_ref, g_ref, o_ref, hg_ref, sq_ref, xbuf, sem):
    i, j = pl.program_id(0), pl.program_id(1)
    n_i = pl.num_programs(0)
    tm = xbuf.shape[1]
    slot = i % 2

    def x_copy(block, s):
        rows = pl.ds(pl.multiple_of(block * tm, tm), tm)
        return pltpu.make_async_copy(x_hbm.at[rows, :], xbuf.at[s], sem.at[s])

    @pl.when(jnp.logical_and(i == 0, j == 0))
    def _():
        x_copy(0, 0).start()

    @pl.when(j == 0)
    def _():
        x_copy(i, slot).wait()

    @pl.when(jnp.logical_and(j == 0, i + 1 < n_i))
    def _():
        x_copy(i + 1, 1 - slot).start()

    h_new = h_ref[...] + _dot(xbuf[slot], w_ref[...])
    o_ref[...] = h_new
    _emit_stats(h_new, g_ref, hg_ref, sq_ref, j == 0)


def _stats_out(m, n, tm, tn, tile_index, row_index):
    specs = [pl.BlockSpec((tm, tn), tile_index), pl.BlockSpec((tm, tn), tile_index),
             pl.BlockSpec((tm, LANES), row_index)]
    shapes = [jax.ShapeDtypeStruct((m, n), F32), jax.ShapeDtypeStruct((m, n), BF16),
              jax.ShapeDtypeStruct((m, LANES), F32)]
    return specs, shapes


def residual_matmul(x, w, layer, h, gain, tm, tn, name):
    m, k = x.shape
    n = w.shape[2]
    tm, tn = _tile(m, tm), _tile(n, tn)
    out_specs, out_shape = _stats_out(m, n, tm, tn, lambda i, j: (i, j), lambda i, j: (i, 0))
    return pl.pallas_call(
        _residual_kernel,
        grid=(m // tm, n // tn),
        in_specs=[pl.BlockSpec(memory_space=pl.ANY),
                  pl.BlockSpec((None, k, tn), lambda i, j: (layer, 0, j)),
                  pl.BlockSpec((tm, tn), lambda i, j: (i, j)),
                  pl.BlockSpec((1, tn), lambda i, j: (0, j))],
        out_specs=out_specs,
        out_shape=out_shape,
        scratch_shapes=[pltpu.VMEM((2, tm, k), x.dtype), pltpu.SemaphoreType.DMA((2,))],
        compiler_params=_cparams(("arbitrary", "arbitrary")),
        name=name,
    )(x, w, h, gain.reshape(1, n))


def _down_kernel(u_ref, w_ref, h_ref, g_ref, o_ref, hg_ref, sq_ref, acc_ref):
    kk = pl.program_id(2)
    last = pl.num_programs(2) - 1
    first_tile = pl.program_id(1) == 0

    @pl.when(kk == 0)
    def _():
        acc_ref[...] = h_ref[...] + _dot(u_ref[...], w_ref[...])

    @pl.when(jnp.logical_and(kk > 0, kk < last))
    def _():
        acc_ref[...] += _dot(u_ref[...], w_ref[...])

    @pl.when(kk == last)
    def _():
        h_new = acc_ref[...] + _dot(u_ref[...], w_ref[...])
        o_ref[...] = h_new
        _emit_stats(h_new, g_ref, hg_ref, sq_ref, first_tile)


def down_proj(u, w, layer, h, gain, tm, tn, tk):
    m, k = u.shape
    n = w.shape[2]
    tm, tn, tk = _tile(m, tm), _tile(n, tn), _tile(k, tk)
    assert k // tk >= 2
    out_specs, out_shape = _stats_out(m, n, tm, tn, lambda i, j, kk: (i, j),
                                      lambda i, j, kk: (i, 0))
    return pl.pallas_call(
        _down_kernel,
        grid=(m // tm, n // tn, k // tk),
        in_specs=[pl.BlockSpec((tm, tk), lambda i, j, kk: (i, kk)),
                  pl.BlockSpec((None, tk, tn), lambda i, j, kk: (layer, kk, j)),
                  pl.BlockSpec((tm, tn), lambda i, j, kk: (i, j)),
                  pl.BlockSpec((1, tn), lambda i, j, kk: (0, j))],
        out_specs=out_specs,
        out_shape=out_shape,
        scratch_shapes=[pltpu.VMEM((tm, tn), F32)],
        compiler_params=_cparams(("parallel", "arbitrary", "arbitrary")),
        name="down_proj",
    )(u, w, h, gain.reshape(1, n))


def _merge_kernel(ya_ref, yb_ref, wa_ref, wb_ref, ga_ref, gb_ref, o_ref):
    a = _dot(ya_ref[...], wa_ref[...])
    b = _dot(yb_ref[...], wb_ref[...])
    ga = _sigmoid(ga_ref[...].astype(F32))
    gb = _sigmoid(gb_ref[...].astype(F32))
    o_ref[...] = (ga * a + gb * b).astype(o_ref.dtype)


def merge(ya, yb, wa, wb, layer, proj, ga_col, gb_col, tm, tn):
    m, k = ya.shape
    n = wa.shape[2]
    tm, tn = _tile(m, tm), _tile(n, tn)
    ga_blk, gb_blk = ga_col // tn, gb_col // tn
    assert ga_blk * tn == ga_col and gb_blk * tn == gb_col
    return pl.pallas_call(
        _merge_kernel,
        grid=(m // tm, n // tn),
        in_specs=[pl.BlockSpec((tm, k), lambda i, j: (i, 0)),
                  pl.BlockSpec((tm, k), lambda i, j: (i, 0)),
                  pl.BlockSpec((None, k, tn), lambda i, j: (layer, 0, j)),
                  pl.BlockSpec((None, k, tn), lambda i, j: (layer, 0, j)),
                  pl.BlockSpec((tm, tn), lambda i, j: (i, ga_blk + j)),
                  pl.BlockSpec((tm, tn), lambda i, j: (i, gb_blk + j))],
        out_specs=pl.BlockSpec((tm, tn), lambda i, j: (i, j)),
        out_shape=jax.ShapeDtypeStruct((m, n), BF16),
        compiler_params=_cparams(("parallel", "parallel")),
        name="merge",
    )(ya, yb, wa, wb, proj, proj)


def _mlstm_sequence(qk_ref, v_ref, og_ref, gc_ref, gr_ref, convw_ref, bc_ref, br_ref,
                    o_ref, xbuf, c_st, n_st, m_st):
    L = MLSTM_CHUNK

    @pl.when(pl.program_id(0) == 0)
    def _():
        xbuf[0:CONV_HALO, :] = jnp.zeros((CONV_HALO, 2 * QK), F32)
        c_st[...] = jnp.zeros_like(c_st)
        n_st[...] = jnp.zeros_like(n_st)
        m_st[...] = jnp.zeros_like(m_st)

    xbuf[CONV_HALO:CONV_HALO + L, :] = qk_ref[...].astype(F32)
    w = convw_ref[...]
    acc = w[CONV_W - 1:CONV_W, :] * xbuf[CONV_HALO:CONV_HALO + L, :]
    for j in range(CONV_W - 1):
        back = CONV_W - 1 - j
        acc = acc + w[j:j + 1, :] * xbuf[CONV_HALO - back:CONV_HALO - back + L, :]
    xbuf[0:CONV_HALO, :] = xbuf[L:L + CONV_HALO, :]
    qk = acc * _sigmoid(acc)

    g_col = gc_ref[...] + bc_ref[...]
    g_row = gr_ref[...] + br_ref[...]
    b_col = _tri_left(_tri(L, True), _log_sigmoid(g_col))
    b_row = _tri_right(_log_sigmoid(g_row), _tri(L, False))

    row = lax.broadcasted_iota(jnp.int32, (L, L), 0)
    col = lax.broadcasted_iota(jnp.int32, (L, L), 1)
    causal = row >= col

    for h in range(HEADS):
        q_f = qk[:, h * DK:(h + 1) * DK]
        q = q_f.astype(BF16)
        k_f = qk[:, QK + h * DK:QK + (h + 1) * DK] * (DK ** -0.5)
        k = k_f.astype(BF16)
        v = v_ref[:, h * DV:(h + 1) * DV].astype(BF16)
        v_f = v.astype(F32)
        li_c = g_col[:, h:h + 1]
        li_r = g_row[h:h + 1, :]
        b_c = b_col[:, HEADS + h:HEADS + h + 1]
        b_r = b_row[HEADS + h:HEADS + h + 1, :]
        m_prev = m_st[h:h + 1, 0:1]
        ct = c_st[h]
        n_row = n_st[h:h + 1, :]

        log_d = jnp.where(causal, b_c - b_r + li_r, -jnp.inf)
        m_t = jnp.maximum(b_c + m_prev, jnp.max(log_d, axis=1, keepdims=True))
        s = _dot_nt(q, k) * jnp.exp(log_d - m_t)
        carry_w = jnp.exp(b_c + m_prev - m_t)
        num = _dot(s.astype(BF16), v) + carry_w * _dot(q, ct.astype(BF16))
        qn = jnp.sum(q_f * n_row, axis=1, keepdims=True)
        den = jnp.sum(s, axis=1, keepdims=True) + carry_w * qn
        hh = num / jnp.maximum(jnp.abs(den), jnp.exp(-m_t))
        og = og_ref[:, h * DV:(h + 1) * DV].astype(F32)
        o_ref[:, h * DV:(h + 1) * DV] = (hh * _sigmoid(og)).astype(o_ref.dtype)

        g_tot = b_c[L - 1:L, :]
        log_w = g_tot - b_c + li_c
        m_new = jnp.maximum(g_tot + m_prev, jnp.max(log_w, axis=0, keepdims=True))
        wgt = jnp.exp(log_w - m_new)
        decay = jnp.exp(g_tot + m_prev - m_new)
        kv = _dot_tn(k, (wgt * v_f).astype(BF16))
        c_st[h] = decay * ct + kv
        n_st[h:h + 1, :] = decay * n_row + jnp.sum(wgt * k_f, axis=0, keepdims=True)
        m_st[h:h + 1, :] = jnp.broadcast_to(m_new, (1, m_st.shape[1]))


def _ride_along_specs(src, layer, steps, step_index=lambda s: s):
    _, r, c = src.shape
    rb = next(rb for rb in range(16, r + 1, 16) if r % rb == 0 and r // rb <= steps)
    last = r // rb - 1
    slab = lambda *g: jnp.minimum(step_index(*g), last)
    in_spec = pl.BlockSpec((None, rb, c), lambda *g: (layer, slab(*g), 0))
    out_spec = pl.BlockSpec((rb, c), lambda *g: (slab(*g), 0))
    return in_spec, out_spec, jax.ShapeDtypeStruct((r, c), BF16)


def _mlstm_kernel(qk_ref, v_ref, og_ref, gc_ref, gr_ref, convw_ref, bc_ref, br_ref, wsrc_ref,
                  o_ref, wdst_ref, xbuf, c_st, n_st, m_st):
    wdst_ref[...] = wsrc_ref[...].astype(wdst_ref.dtype)
    for b in range(qk_ref.shape[0]):
        _mlstm_sequence(qk_ref.at[b], v_ref.at[b], og_ref.at[b], gc_ref.at[b], gr_ref.at[b],
                        convw_ref, bc_ref, br_ref, o_ref.at[b], xbuf.at[b], c_st.at[b],
                        n_st.at[b], m_st.at[b])


def mlstm_mixer(proj, small, small_t, conv_w, bias_col, bias_row, qk_col, v_col, og_col,
                cast_src, layer):
    bsz, t, _ = proj.shape
    L = MLSTM_CHUNK
    assert t % L == 0
    qk_blk, v_blk, og_blk = qk_col // (2 * QK), v_col // VW, og_col // VW
    cast_in, cast_out, cast_shape = _ride_along_specs(cast_src, layer, t // L)
    return pl.pallas_call(
        _mlstm_kernel,
        grid=(t // L,),
        in_specs=[pl.BlockSpec((bsz, L, 2 * QK), lambda c: (0, c, qk_blk)),
                  pl.BlockSpec((bsz, L, VW), lambda c: (0, c, v_blk)),
                  pl.BlockSpec((bsz, L, VW), lambda c: (0, c, og_blk)),
                  pl.BlockSpec((bsz, L, SMALL_W), lambda c: (0, c, 0)),
                  pl.BlockSpec((bsz, 8, L), lambda c: (0, 0, c)),
                  pl.BlockSpec((CONV_W, 2 * QK), lambda c: (0, 0)),
                  pl.BlockSpec((1, SMALL_W), lambda c: (0, 0)),
                  pl.BlockSpec((8, L), lambda c: (0, 0)),
                  cast_in],
        out_specs=[pl.BlockSpec((bsz, L, VW), lambda c: (0, c, 0)), cast_out],
        out_shape=[jax.ShapeDtypeStruct((bsz, t, VW), BF16), cast_shape],
        scratch_shapes=[pltpu.VMEM((bsz, L + CONV_HALO, 2 * QK), F32),
                        pltpu.VMEM((bsz, HEADS, DK, DV), F32),
                        pltpu.VMEM((bsz, 8, DK), F32),
                        pltpu.VMEM((bsz, 8, 128), F32)],
        compiler_params=_cparams(("arbitrary",)),
        name="mlstm",
    )(proj, proj, proj, small, small_t, conv_w, bias_col, bias_row, cast_src)


def _half_block_ref(x, w):
    n_rows, n = x.shape
    if 2 * w >= 8:
        x3 = x.reshape(n_rows // (2 * w), 2 * w, n)
        return jnp.broadcast_to(x3[:, w - 1:w, :], x3.shape).reshape(n_rows, n)
    x3 = x.reshape(n_rows // 8, 8, n)
    sub = lax.broadcasted_iota(jnp.int32, x3.shape, 1)
    groups = 8 // (2 * w)
    pick = lambda g: jnp.broadcast_to(x3[:, g * 2 * w + w - 1:g * 2 * w + w, :], x3.shape)
    out = pick(groups - 1)
    for g in reversed(range(groups - 1)):
        out = jnp.where(sub < (g + 1) * 2 * w, pick(g), out)
    return out.reshape(n_rows, n)


def _gla_kernel(qk_ref, v_ref, og_ref, gc_ref, wg_ref, bg_ref, gn_ref, wsrc_ref,
                o_ref, wdst_ref, s_st):
    L = GLA_CHUNK
    seqs = range(qk_ref.shape[0])
    heads = range(HEADS)
    n_levels = L.bit_length() - 1
    lanes = lambda h, w: slice(h * w, (h + 1) * w)

    wdst_ref[...] = wsrc_ref[...].astype(wdst_ref.dtype)

    @pl.when(pl.program_id(0) == 0)
    def _():
        s_st[...] = jnp.zeros_like(s_st)

    row = lax.broadcasted_iota(jnp.int32, (L, L), 0)
    col = lax.broadcasted_iota(jnp.int32, (L, L), 1)
    level = jnp.where(row > col, 31 - lax.clz(row ^ col), -1)
    diag = row == col
    eye = (lax.broadcasted_iota(jnp.int32, (DK, DK), 0)
           == lax.broadcasted_iota(jnp.int32, (DK, DK), 1))
    tril = _tri(L, True)

    cum, q_f, k_f = [], [], []
    for b in seqs:
        z = _dot(gc_ref[b].astype(BF16), wg_ref[...]) + bg_ref[...]
        cum.append(_tri_left(tril, _log_sigmoid(z) * (1.0 / GATE_TAU)))
        q_f.append(qk_ref[b, :, 0:QK].astype(F32) * (DK ** -0.5))
        k_f.append(qk_ref[b, :, QK:2 * QK].astype(F32))

    p = {}
    for b in seqs:
        for lw in range(n_levels):
            f = jnp.exp(-jnp.abs(cum[b] - _half_block_ref(cum[b], 1 << lw)))
            qf = (q_f[b] * f).astype(BF16)
            kf = (k_f[b] * f).astype(BF16)
            for h in heads:
                p[b, h, lw] = _dot_nt(qf[:, lanes(h, DK)], kf[:, lanes(h, DK)])

    q_hat, k_dec, last = [], [], []
    for b in seqs:
        last.append(cum[b][L - 1:L, :])
        q_hat.append((q_f[b] * jnp.exp(cum[b])).astype(BF16))
        k_dec.append((k_f[b] * jnp.exp(last[b] - cum[b])).astype(BF16))

    for b in seqs:
        qk_prod = q_f[b] * k_f[b]
        for h in heads:
            att = jnp.where(diag, jnp.sum(qk_prod[:, lanes(h, DK)], axis=1, keepdims=True), 0.0)
            for lw in range(n_levels):
                att = att + jnp.where(level == lw, p[b, h, lw], 0.0)

            v = v_ref[b, :, lanes(h, DV)].astype(BF16)
            st = s_st[b, h]
            o = _dot(att.astype(BF16), v) + _dot(q_hat[b][:, lanes(h, DK)], st.astype(BF16))
            kv = _dot_tn(k_dec[b][:, lanes(h, DK)], v)
            decay_col = jnp.sum(jnp.where(eye, jnp.exp(last[b][:, lanes(h, DK)]), 0.0),
                                axis=1, keepdims=True)
            s_st[b, h] = decay_col * st + kv

            o = o * lax.rsqrt(jnp.mean(o * o, axis=1, keepdims=True) + EPS)
            og = og_ref[b, :, lanes(h, DV)].astype(F32)
            y = o * gn_ref[:, lanes(h, DV)] * (og * _sigmoid(og))
            o_ref[b, :, lanes(h, DV)] = y.astype(o_ref.dtype)


def gla_mixer(proj, small, wg_pad, bg, gn, qk_col, v_col, og_col, cast_src, layer):
    bsz, t, _ = proj.shape
    L = GLA_CHUNK
    assert t % L == 0
    qk_blk, v_blk, og_blk = qk_col // (2 * QK), v_col // VW, og_col // VW
    cast_in, cast_out, cast_shape = _ride_along_specs(cast_src, layer, t // L)
    return pl.pallas_call(
        _gla_kernel,
        grid=(t // L,),
        in_specs=[pl.BlockSpec((bsz, L, 2 * QK), lambda c: (0, c, qk_blk)),
                  pl.BlockSpec((bsz, L, VW), lambda c: (0, c, v_blk)),
                  pl.BlockSpec((bsz, L, VW), lambda c: (0, c, og_blk)),
                  pl.BlockSpec((bsz, L, SMALL_W), lambda c: (0, c, 0)),
                  pl.BlockSpec((SMALL_W, QK), lambda c: (0, 0)),
                  pl.BlockSpec((1, QK), lambda c: (0, 0)),
                  pl.BlockSpec((1, VW), lambda c: (0, 0)),
                  cast_in],
        out_specs=[pl.BlockSpec((bsz, L, VW), lambda c: (0, c, 0)), cast_out],
        out_shape=[jax.ShapeDtypeStruct((bsz, t, VW), BF16), cast_shape],
        scratch_shapes=[pltpu.VMEM((bsz, HEADS, DK, DV), F32)],
        compiler_params=_cparams(("arbitrary",)),
        name="gla",
    )(proj, proj, proj, small, wg_pad, bg, gn, cast_src)


def _projection_layout(d_model):
    a_w = 2 * QK + 2 * VW
    src_a = 0
    src_if = a_w
    src_b = src_if + 2 * HEADS
    src_z = src_b + a_w
    src_g = src_z + GATE_RANK
    end = src_g + 2 * d_model
    main = [(src_a, src_if), (src_b, src_z), (src_g, end)]
    small = [(src_if, src_b), (src_z, src_g)]
    cols = dict(a_qk=0, a_v=2 * QK, a_og=2 * QK + VW, b_qk=a_w, b_v=a_w + 2 * QK,
                b_og=a_w + 2 * QK + VW, ga=2 * a_w, gb=2 * a_w + d_model)
    return main, small, cols, end


def _layer(h, hg, ssq, w_in_packed, w_out, bsz, t_pad, l, p, w, cols, tiles, next_gain, last):
    m, d = h.shape
    if last:
        next_packed = None
        proj = normed_matmul(hg, ssq, w_in_packed[None], 0, BF16, tiles['tm'], tiles['tn_in'],
                             w_is_nk=True, relu2=False, name="in_proj")
    else:
        proj, next_packed = in_proj_pack(hg, ssq, w_in_packed, w['in_t_f32'], l + 1,
                                         w['in_segments'], tiles['tm'], tiles['tn_in'])
    small_p = normed_matmul(hg, ssq, w['in_small_t'], l, F32, tiles['tm'], SMALL_W,
                            w_is_nk=True, relu2=False, name="in_proj_small")

    proj3 = proj.reshape(bsz, t_pad, proj.shape[1])
    small3 = small_p.reshape(bsz, t_pad, SMALL_W)
    small_t = jnp.swapaxes(small3[:, :, :2 * HEADS], 1, 2)
    bias_col = jnp.pad(p['b_if'], (0, SMALL_W - 2 * HEADS)).reshape(1, SMALL_W)
    bias_row = jnp.broadcast_to(p['b_if'].reshape(2 * HEADS, 1), (2 * HEADS, MLSTM_CHUNK))
    ya, w_down = mlstm_mixer(proj3, small3, small_t, p['conv_qk'], bias_col, bias_row,
                             cols['a_qk'], cols['a_v'], cols['a_og'], w['down_f32'], l)

    wg_pad = jnp.zeros((SMALL_W, QK), F32).at[2 * HEADS:2 * HEADS + GATE_RANK].set(p['w_gla_gate'])
    yb, w_up = gla_mixer(proj3, small3, wg_pad.astype(BF16), p['b_gla_gate'].reshape(1, QK),
                         p['norm_gla'].reshape(1, VW), cols['b_qk'], cols['b_v'], cols['b_og'],
                         w['up_f32'], l)

    merged = merge(ya.reshape(m, VW), yb.reshape(m, VW), w['br_a'], w['br_b'], l, proj,
                   cols['ga'], cols['gb'], tiles['tm'], tiles['tn_merge'])
    h, hg, ssq = residual_matmul(merged, w_out[None], 0, h, p['norm_mlp'], tiles['tm'],
                                 tiles['tn_out'], name="out_proj")
    if last:
        next_out = None
        u = normed_matmul(hg, ssq, w_up[None], 0, BF16, tiles['tm'], tiles['tn_up'],
                          w_is_nk=False, relu2=True, name="up_proj")
    else:
        u, next_out = up_proj_cast(hg, ssq, w_up, w['out_f32'], l + 1, tiles['tm'],
                                   tiles['tn_up'])
    h, hg, ssq = down_proj(u, w_down[None], 0, h, next_gain, tiles['tm'], tiles['tn_down'],
                           tiles['tk'])
    return h, hg, ssq, next_packed, next_out


def kernel(x, meta, norm_mix, w_in, conv_qk, b_if, w_gla_gate, b_gla_gate, norm_gla, w_br_a,
           w_br_b, w_out, norm_mlp, w_up, w_down, norm_final):
    bsz, seq, d = x.shape
    depth = w_in.shape[0]
    n = N_META + seq
    t_pad = -(-n // T_ALIGN) * T_ALIGN
    assert meta.shape[0] == N_META
    h, hg, ssq = embed_stats(x, meta.astype(x.dtype), norm_mix[0], t_pad)

    main, small, cols, p_in = _projection_layout(d)
    assert w_in.shape[2] == p_in
    w_in_t = jnp.swapaxes(w_in, 1, 2)
    w = dict(in_t_f32=w_in_t, in_segments=main,
             in_small_t=gather_rows(w_in_t, small, SMALL_W), br_a=w_br_a.astype(BF16),
             br_b=w_br_b.astype(BF16), out_f32=w_out, up_f32=w_up, down_f32=w_down)
    w_in_packed = pack_rows(w_in_t, 0, main, tile=512)
    w_out_l = w_out[0].astype(BF16)

    tiles = dict(tm=1280, tn_in=1024, tn_merge=1024, tn_out=512, tn_up=1024, tn_down=1024, tk=2048)
    for l in range(depth):
        p = dict(conv_qk=conv_qk[l], b_if=b_if[l], w_gla_gate=w_gla_gate[l],
                 b_gla_gate=b_gla_gate[l], norm_gla=norm_gla[l], norm_mlp=norm_mlp[l])
        next_gain = norm_mix[l + 1] if l + 1 < depth else norm_final
        h, hg, ssq, w_in_packed, w_out_l = _layer(h, hg, ssq, w_in_packed, w_out_l, bsz, t_pad, l,
                                                  p, w, cols, tiles, next_gain,
                                                  last=l + 1 == depth)
    return final_norm(h, norm_final, bsz, N_META, seq)
```
